```python
import math
import jax, jax.numpy as jnp
from jax import lax
import numpy as np

D_MODEL = 1024
BATCH = 4
SEQ = 8192
DEPTH = 1

HEAD_DIM = 64
NSA_HEADS = 8
NSA_KV_GROUPS = 2
NSA_GROUP_SIZE = NSA_HEADS // NSA_KV_GROUPS
CMP_BLOCK = 32
CMP_STRIDE = 16
CMP_HIDDEN = 128
SEL_BLOCK = 64
SEL_TOP_N = 16
WINDOW = 512
Q_BLOCK = 128
FORCE_BONUS = 1.0e4
ROPE_THETA = 500000.0
ROPE_DIM = HEAD_DIM // 4

SGU_GROUPS = 4
SGU_CHUNK = 128
SGU_WIDTH = 512
SGU_GROUP_DIM = SGU_WIDTH // SGU_GROUPS

FFN_HIDDEN = -(-(8 * D_MODEL) // (3 * 256)) * 256

Q_WIDTH = NSA_HEADS * HEAD_DIM
KV_WIDTH = NSA_KV_GROUPS * HEAD_DIM
NSA_GATE_WIDTH = 3 * NSA_HEADS
MERGE_GATE_WIDTH = 2 * D_MODEL
IN_PROJ_WIDTH = Q_WIDTH + 6 * KV_WIDTH + NSA_GATE_WIDTH + 2 * SGU_WIDTH + MERGE_GATE_WIDTH

DEEPNORM_ALPHA = (2.0 * DEPTH) ** 0.25
DEEPNORM_BETA = (8.0 * DEPTH) ** -0.25
LN_EPS = 1e-5
NEG_INF = -1e30

kernel_name = "nsa_gmlp_hybrid_deepnorm"


def _split_points():
    sizes = [Q_WIDTH] + [KV_WIDTH] * 6 + [NSA_GATE_WIDTH, SGU_WIDTH, SGU_WIDTH, MERGE_GATE_WIDTH]
    return [int(v) for v in np.cumsum(sizes)[:-1]]


def layer_norm(x, g, b):
    xf = x.astype(jnp.float32)
    mu = jnp.mean(xf, axis=-1, keepdims=True)
    var = jnp.mean(jnp.square(xf - mu), axis=-1, keepdims=True)
    y = (xf - mu) * lax.rsqrt(var + LN_EPS) * g.astype(jnp.float32) + b.astype(jnp.float32)
    return y.astype(x.dtype)


def partial_rope(x, positions):
    half = ROPE_DIM // 2
    freqs = ROPE_THETA ** (-jnp.arange(half, dtype=jnp.float32) / half)
    ang = positions.astype(jnp.float32)[..., None] * freqs
    cos = jnp.cos(ang)[:, :, None, :]
    sin = jnp.sin(ang)[:, :, None, :]
    xr = x[..., :ROPE_DIM].astype(jnp.float32)
    x1, x2 = xr[..., :half], xr[..., half:]
    rot = jnp.concatenate([x1 * cos - x2 * sin, x2 * cos + x1 * sin], axis=-1).astype(x.dtype)
    return jnp.concatenate([rot, x[..., ROPE_DIM:]], axis=-1)


def masked_softmax(s, mask):
    s = jnp.where(mask, s.astype(jnp.float32), NEG_INF)
    p = jax.nn.softmax(s, axis=-1)
    return jnp.where(mask, p, 0.0)


def compress(tokens, pe, w1, w2):
    B, S, G, dh = tokens.shape
    c = tokens.reshape(B, S // CMP_STRIDE, CMP_STRIDE, G, dh)
    blocks = jnp.concatenate([c[:, :-1], c[:, 1:]], axis=2)
    blocks = blocks + pe[None, None, :, None, :]
    nc = blocks.shape[1]
    flat = blocks.transpose(0, 1, 3, 2, 4).reshape(B, nc, G, CMP_BLOCK * dh)
    return jax.nn.silu(flat @ w1) @ w2


def native_sparse_attention(q, k_cmp, v_cmp, k_sel, v_sel, k_win, v_win, gates,
                            pe_ck, w_ck1, w_ck2, pe_cv, w_cv1, w_cv2):
    B, S, H, dh = q.shape
    G, R = NSA_KV_GROUPS, NSA_GROUP_SIZE
    kc = compress(k_cmp, pe_ck, w_ck1, w_ck2)
    vc = compress(v_cmp, pe_cv, w_cv1, w_cv2)
    NC = kc.shape[1]
    NS = S // SEL_BLOCK
    n_sel = min(SEL_TOP_N, NS)
    NQ = S // Q_BLOCK
    cmp_end = jnp.arange(NC) * CMP_STRIDE + CMP_BLOCK - 1
    cs = jnp.arange(NC)[:, None] * CMP_STRIDE
    ss = jnp.arange(NS)[None, :] * SEL_BLOCK
    overlap = jnp.clip(jnp.minimum(cs + CMP_BLOCK, ss + SEL_BLOCK) - jnp.maximum(cs, ss), 0, None).astype(jnp.float32) / CMP_BLOCK
    ks_blocks = k_sel.reshape(B, NS, SEL_BLOCK, G, dh).transpose(0, 3, 1, 2, 4)
    vs_blocks = v_sel.reshape(B, NS, SEL_BLOCK, G, dh).transpose(0, 3, 1, 2, 4)
    kw_pad = jnp.pad(k_win, ((0, 0), (WINDOW, 0), (0, 0), (0, 0)))
    vw_pad = jnp.pad(v_win, ((0, 0), (WINDOW, 0), (0, 0), (0, 0)))
    b_idx = jnp.arange(B)[:, None, None, None]
    g_idx = jnp.arange(G)[None, :, None, None]
    scale = HEAD_DIM ** -0.5
    sel_j = jnp.arange(NS)

    def one_block(c):
        start = c * Q_BLOCK
        t = start + jnp.arange(Q_BLOCK)
        qb = lax.dynamic_slice_in_dim(q, start, Q_BLOCK, axis=1).reshape(B, Q_BLOCK, G, R, dh) * scale
        gb = lax.dynamic_slice_in_dim(gates, start, Q_BLOCK, axis=1).reshape(B, Q_BLOCK, G, R, 3)
        s_c = jnp.einsum('bqgrd,bngd->bgrqn', qb, kc)
        p_c = masked_softmax(s_c, cmp_end[None, :] <= t[:, None])
        o_cmp = jnp.einsum('bgrqn,bngd->bqgrd', p_c.astype(vc.dtype), vc)
        imp = jnp.einsum('bgrqn,nj->bgqj', p_c, overlap)
        cur = t // SEL_BLOCK
        valid = sel_j[None, :] <= cur[:, None]
        forced = (sel_j[None, :] == 0) | (sel_j[None, :] == cur[:, None]) | (sel_j[None, :] == cur[:, None] - 1)
        score = jnp.where(valid, imp + jnp.where(forced, FORCE_BONUS, 0.0), NEG_INF)
        top_val, top_idx = lax.top_k(score, n_sel)
        top_ok = top_val > 0.5 * NEG_INF
        gk = ks_blocks[b_idx, g_idx, top_idx]
        gv = vs_blocks[b_idx, g_idx, top_idx]
        kpos = top_idx[..., None] * SEL_BLOCK + jnp.arange(SEL_BLOCK)
        m_s = top_ok[..., None] & (kpos <= t[None, None, :, None, None])
        s_s = jnp.einsum('bqgrd,bgqnkd->bgrqnk', qb, gk).reshape(B, G, R, Q_BLOCK, n_sel * SEL_BLOCK)
        p_s = masked_softmax(s_s, m_s.reshape(B, G, 1, Q_BLOCK, n_sel * SEL_BLOCK))
        o_sel = jnp.einsum('bgrqm,bgqmd->bqgrd', p_s.astype(gv.dtype), gv.reshape(B, G, Q_BLOCK, n_sel * SEL_BLOCK, dh))
        kw = lax.dynamic_slice_in_dim(kw_pad, start, WINDOW + Q_BLOCK, axis=1)
        vw = lax.dynamic_slice_in_dim(vw_pad, start, WINDOW + Q_BLOCK, axis=1)
        wpos = start - WINDOW + jnp.arange(WINDOW + Q_BLOCK)
        m_w = (wpos[None, :] <= t[:, None]) & (wpos[None, :] > t[:, None] - WINDOW) & (wpos[None, :] >= 0)
        s_w = jnp.einsum('bqgrd,bkgd->bgrqk', qb, kw)
        p_w = masked_softmax(s_w, m_w)
        o_win = jnp.einsum('bgrqk,bkgd->bqgrd', p_w.astype(vw.dtype), vw)
        out = gb[..., 0:1] * o_cmp + gb[..., 1:2] * o_sel + gb[..., 2:3] * o_win
        return out.reshape(B, Q_BLOCK, H * dh)

    outs = lax.map(one_block, jnp.arange(NQ))
    return outs.transpose(1, 0, 2, 3).reshape(B, S, H * dh)


def spatial_gating(u, v, ln_g, ln_b, w_s, b_s):
    B, S, _ = u.shape
    u = jax.nn.gelu(u)
    v = layer_norm(jax.nn.gelu(v), ln_g, ln_b)
    vc = v.reshape(B, S // SGU_CHUNK, SGU_CHUNK, SGU_GROUPS, SGU_GROUP_DIM)
    causal = jnp.tril(jnp.ones((SGU_CHUNK, SGU_CHUNK), dtype=bool))
    w = jnp.where(causal, w_s, 0.0).astype(v.dtype)
    mixed = jnp.einsum('gts,bnsgc->bntgc', w, vc) + b_s.T[None, None, :, :, None]
    return u * mixed.reshape(B, S, SGU_WIDTH)


def setup_inputs(seed: int = 0) -> dict:
    key = jax.random.key(seed)
    ks = jax.random.split(key, 24)
    L = DEPTH

    def nrm(k, shape, scale):
        return jax.random.normal(k, shape, jnp.float32) * scale

    x = nrm(ks[0], (BATCH, SEQ, D_MODEL), 1.0)
    positions = jnp.arange(SEQ, dtype=jnp.int32)[None, :] + jax.random.randint(ks[1], (BATCH, 1), 0, 1024, dtype=jnp.int32)
    return {
        'x': x,
        'positions': positions,
        'w_in': nrm(ks[2], (L, D_MODEL, IN_PROJ_WIDTH), D_MODEL ** -0.5),
        'pe_ck': nrm(ks[3], (L, CMP_BLOCK, HEAD_DIM), 0.1),
        'w_ck1': nrm(ks[4], (L, CMP_BLOCK * HEAD_DIM, CMP_HIDDEN), (CMP_BLOCK * HEAD_DIM) ** -0.5),
        'w_ck2': nrm(ks[5], (L, CMP_HIDDEN, HEAD_DIM), CMP_HIDDEN ** -0.5),
        'pe_cv': nrm(ks[6], (L, CMP_BLOCK, HEAD_DIM), 0.1),
        'w_cv1': nrm(ks[7], (L, CMP_BLOCK * HEAD_DIM, CMP_HIDDEN), (CMP_BLOCK * HEAD_DIM) ** -0.5),
        'w_cv2': nrm(ks[8], (L, CMP_HIDDEN, HEAD_DIM), CMP_HIDDEN ** -0.5),
        'ln_sgu_g': 1.0 + nrm(ks[9], (L, SGU_WIDTH), 0.1),
        'ln_sgu_b': nrm(ks[10], (L, SGU_WIDTH), 0.1),
        'w_spatial': nrm(ks[11], (L, SGU_GROUPS, SGU_CHUNK, SGU_CHUNK), SGU_CHUNK ** -0.5),
        'b_spatial': 1.0 + nrm(ks[12], (L, SGU_GROUPS, SGU_CHUNK), 0.1),
        'w_branch_nsa': nrm(ks[13], (L, Q_WIDTH, D_MODEL), DEEPNORM_BETA * Q_WIDTH ** -0.5),
        'w_branch_sgu': nrm(ks[14], (L, SGU_WIDTH, D_MODEL), DEEPNORM_BETA * SGU_WIDTH ** -0.5),
        'w_out': nrm(ks[15], (L, D_MODEL, D_MODEL), DEEPNORM_BETA * D_MODEL ** -0.5),
        'ln1_g': 1.0 + nrm(ks[16], (L, D_MODEL), 0.1),
        'ln1_b': nrm(ks[17], (L, D_MODEL), 0.1),
        'w_ffn_gate': nrm(ks[18], (L, D_MODEL, FFN_HIDDEN), DEEPNORM_BETA * D_MODEL ** -0.5),
        'w_ffn_up': nrm(ks[19], (L, D_MODEL, FFN_HIDDEN), DEEPNORM_BETA * D_MODEL ** -0.5),
        'w_ffn_down': nrm(ks[20], (L, FFN_HIDDEN, D_MODEL), DEEPNORM_BETA * FFN_HIDDEN ** -0.5),
        'ln2_g': 1.0 + nrm(ks[21], (L, D_MODEL), 0.1),
        'ln2_b': nrm(ks[22], (L, D_MODEL), 0.1),
    }


def reference(x, positions, w_in, pe_ck, w_ck1, w_ck2, pe_cv, w_cv1, w_cv2, ln_sgu_g, ln_sgu_b,
              w_spatial, b_spatial, w_branch_nsa, w_branch_sgu, w_out, ln1_g, ln1_b,
              w_ffn_gate, w_ffn_up, w_ffn_down, ln2_g, ln2_b):
    B, S, _ = x.shape
    splits = _split_points()
    h = x
    for l in range(DEPTH):
        proj = h @ w_in[l]
        q, k_c, v_c, k_s, v_s, k_w, v_w, nsa_gl, u, v, merge_gl = jnp.split(proj, splits, axis=-1)
        q = partial_rope(q.reshape(B, S, NSA_HEADS, HEAD_DIM), positions)
        kv = lambda a: a.reshape(B, S, NSA_KV_GROUPS, HEAD_DIM)
        k_c = partial_rope(kv(k_c), positions)
        k_s = partial_rope(kv(k_s), positions)
        k_w = partial_rope(kv(k_w), positions)
        nsa_gates = jax.nn.sigmoid(nsa_gl).reshape(B, S, NSA_HEADS, 3)
        o_nsa = native_sparse_attention(q, k_c, kv(v_c), k_s, kv(v_s), k_w, kv(v_w), nsa_gates,
                                        pe_ck[l], w_ck1[l], w_ck2[l], pe_cv[l], w_cv1[l], w_cv2[l])
        o_sgu = spatial_gating(u, v, ln_sgu_g[l], ln_sgu_b[l], w_spatial[l], b_spatial[l])
        g_nsa, g_sgu = jnp.split(jax.nn.sigmoid(merge_gl), 2, axis=-1)
        merged = g_nsa * (o_nsa @ w_branch_nsa[l]) + g_sgu * (o_sgu @ w_branch_sgu[l])
        mix = merged @ w_out[l]
        h = layer_norm(DEEPNORM_ALPHA * h + mix, ln1_g[l], ln1_b[l])
        ffn = (jax.nn.silu(h @ w_ffn_gate[l]) * (h @ w_ffn_up[l])) @ w_ffn_down[l]
        h = layer_norm(DEEPNORM_ALPHA * h + ffn, ln2_g[l], ln2_b[l])
    return h
```

```python
import functools

import numpy as np
import jax
import jax.numpy as jnp
from jax import lax
from jax.experimental import pallas as pl
from jax.experimental.pallas import tpu as pltpu

F32 = jnp.float32
BF16 = jnp.bfloat16

HEAD_DIM = 64
NSA_HEADS = 8
NSA_KV_GROUPS = 2
NSA_GROUP_SIZE = NSA_HEADS // NSA_KV_GROUPS
CMP_BLOCK = 32
CMP_STRIDE = 16
CMP_HIDDEN = 128
SEL_BLOCK = 64
SEL_TOP_N = 16
WINDOW = 512
Q_BLOCK = 128
FORCE_BONUS = 1.0e4
ROPE_THETA = 500000.0
ROPE_DIM = HEAD_DIM // 4
ROPE_HALF = ROPE_DIM // 2
SGU_GROUPS = 4
SGU_CHUNK = 128
SGU_WIDTH = 512
LN_EPS = 1e-5
NEG_INF = -1e30

LANES = 128
KV_PAIR = NSA_KV_GROUPS * HEAD_DIM
assert KV_PAIR == LANES

TOKEN_TILE = 512
SEL_KEY_CHUNK = 512
WIN_KEYS = WINDOW + Q_BLOCK
VMEM_LIMIT = 56 * 1024 * 1024


def _layer_norm(x, g, b):
    mu = jnp.mean(x, axis=-1, keepdims=True)
    xc = x - mu
    var = jnp.mean(xc * xc, axis=-1, keepdims=True)
    return xc * lax.rsqrt(var + LN_EPS) * g + b


def _dot(a, b):
    return jnp.dot(a, b, preferred_element_type=F32)


def _dot_nt(a, b):
    return lax.dot_general(a, b, (((1,), (1,)), ((), ())), preferred_element_type=F32)


Q_WIDTH = NSA_HEADS * HEAD_DIM
COL_Q = 0
COL_KV = COL_Q + Q_WIDTH
COL_U = COL_KV + 6 * KV_PAIR
COL_V = COL_U + SGU_WIDTH
COL_MERGE = COL_V + SGU_WIDTH
ROPED_KV = (0, 2, 4)


def _inproj_kernel(x_ref, pos_ref, freq_ref, w_ref, lng_ref, lnb_ref, wsp_ref, bsp_ref,
                   q_ref, kc_ref, vc_ref, ks_ref, vs_ref, kw_ref, vw_ref,
                   gates_ref, sgu_ref, merge_ref, *, d_model):
    tm = x_ref.shape[0]
    col_gates = COL_MERGE + 2 * d_model
    xb = x_ref[...].astype(BF16)

    ang = pos_ref[...] * freq_ref[...]
    cos_t = jnp.cos(ang)
    sin_t = jnp.sin(ang)
    lane = lax.broadcasted_iota(jnp.int32, (1, LANES), 1)
    d_idx = lane % HEAD_DIM
    sin_lo = jnp.where(d_idx < ROPE_HALF, -sin_t, 0.0)
    sin_hi = jnp.where((d_idx >= ROPE_HALF) & (d_idx < ROPE_DIM), sin_t, 0.0)

    def rope(t):
        return (t * cos_t + pltpu.roll(t, LANES - ROPE_HALF, axis=1) * sin_lo
                + pltpu.roll(t, ROPE_HALF, axis=1) * sin_hi)

    rq = _dot(xb, w_ref[:, COL_Q:COL_Q + Q_WIDTH])
    scale = HEAD_DIM ** -0.5
    for pair in range(NSA_HEADS // 2):
        t = rope(rq[:, pair * LANES:(pair + 1) * LANES]) * scale
        t_sw = pltpu.roll(t, HEAD_DIM, axis=1)
        for half in range(2):
            h = 2 * pair + half
            grp = h // NSA_GROUP_SIZE
            src = t if half == grp else t_sw
            keep = (lane >= HEAD_DIM) if grp == 1 else (lane < HEAD_DIM)
            q_ref[0, h] = jnp.where(keep, src, 0.0).astype(q_ref.dtype)

    rkv = _dot(xb, w_ref[:, COL_KV:COL_KV + 6 * KV_PAIR])
    kv_refs = (kc_ref, vc_ref, ks_ref, vs_ref, kw_ref, vw_ref)
    for i, ref in enumerate(kv_refs):
        t = rkv[:, i * KV_PAIR:(i + 1) * KV_PAIR]
        if i in ROPED_KV:
            t = rope(t)
        ref[0] = t.astype(ref.dtype)

    u = jax.nn.gelu(_dot(xb, w_ref[:, COL_U:COL_U + SGU_WIDTH]))
    v = jax.nn.gelu(_dot(xb, w_ref[:, COL_V:COL_V + SGU_WIDTH]))
    v = _layer_norm(v, lng_ref[...], lnb_ref[...]).astype(BF16)
    row = lax.broadcasted_iota(jnp.int32, (SGU_CHUNK, SGU_CHUNK), 0)
    col = lax.broadcasted_iota(jnp.int32, (SGU_CHUNK, SGU_CHUNK), 1)
    gdim = SGU_WIDTH // SGU_GROUPS
    w_sp = [jnp.where(col <= row, wsp_ref[g], 0.0).astype(BF16) for g in range(SGU_GROUPS)]
    for n in range(tm // SGU_CHUNK):
        rows = slice(n * SGU_CHUNK, (n + 1) * SGU_CHUNK)
        mixed = jnp.concatenate(
            [_dot(w_sp[g], v[rows, g * gdim:(g + 1) * gdim]) for g in range(SGU_GROUPS)], axis=1)
        sgu_ref[rows, :] = (u[rows, :] * (mixed + bsp_ref[...])).astype(sgu_ref.dtype)

    rm = _dot(xb, w_ref[:, COL_MERGE:COL_MERGE + 2 * d_model])
    merge_ref[...] = jax.nn.sigmoid(rm).astype(merge_ref.dtype)
    rg = _dot(xb, w_ref[:, col_gates:col_gates + LANES])
    gates_ref[0] = jax.nn.sigmoid(rg)


def _inproj(x, pos_f, freq_lane, w_all, ln_g, ln_b, w_sp, b_sp):
    B, S, D = x.shape
    tm = TOKEN_TILE
    nt = S // tm
    wcols = w_all.shape[1]
    x2 = x.reshape(B * S, D)
    kv_shape = jax.ShapeDtypeStruct((B, S, KV_PAIR), BF16)
    kv_spec = pl.BlockSpec((1, tm, KV_PAIR), lambda b, i: (b, i, 0))
    const2 = lambda b, i: (0, 0)
    return pl.pallas_call(
        functools.partial(_inproj_kernel, d_model=D),
        grid=(B, nt),
        in_specs=[
            pl.BlockSpec((tm, D), lambda b, i: (b * nt + i, 0)),
            pl.BlockSpec((tm, 1), lambda b, i: (b * nt + i, 0)),
            pl.BlockSpec((1, LANES), const2),
            pl.BlockSpec((D, wcols), const2),
            pl.BlockSpec((1, SGU_WIDTH), const2),
            pl.BlockSpec((1, SGU_WIDTH), const2),
            pl.BlockSpec((SGU_GROUPS, SGU_CHUNK, SGU_CHUNK), lambda b, i: (0, 0, 0)),
            pl.BlockSpec((SGU_CHUNK, SGU_WIDTH), const2),
        ],
        out_specs=[
            pl.BlockSpec((1, NSA_HEADS, tm, LANES), lambda b, i: (b, 0, i, 0)),
            kv_spec, kv_spec, kv_spec, kv_spec, kv_spec, kv_spec,
            pl.BlockSpec((1, tm, LANES), lambda b, i: (b, i, 0)),
            pl.BlockSpec((tm, SGU_WIDTH), lambda b, i: (b * nt + i, 0)),
            pl.BlockSpec((tm, 2 * D), lambda b, i: (b * nt + i, 0)),
        ],
        out_shape=[
            jax.ShapeDtypeStruct((B, NSA_HEADS, S, LANES), BF16),
            kv_shape, kv_shape, kv_shape, kv_shape, kv_shape, kv_shape,
            jax.ShapeDtypeStruct((B, S, LANES), F32),
            jax.ShapeDtypeStruct((B * S, SGU_WIDTH), BF16),
            jax.ShapeDtypeStruct((B * S, 2 * D), BF16),
        ],
        compiler_params=pltpu.CompilerParams(
            dimension_semantics=("parallel", "parallel"), vmem_limit_bytes=VMEM_LIMIT),
        name="inproj",
    )(x2, pos_f, freq_lane, w_all, ln_g, ln_b, w_sp, b_sp)


def _compress_kernel(k_ref, v_ref, kpe_ref, kw1_ref, kw2_ref, vpe_ref, vw1_ref, vw2_ref,
                     ko_ref, vo_ref):
    ncp = k_ref.shape[1]

    def one(tok_ref, pe_ref, w1_ref, w2_ref, out_ref):
        a = tok_ref[0]
        width = pe_ref.shape[1]
        pe_first = jnp.broadcast_to(pe_ref[0:1, :], (8, width)).astype(BF16)
        pe_second = jnp.broadcast_to(pe_ref[1:2, :], (8, width)).astype(BF16)
        outs = []
        for g in range(NSA_KV_GROUPS):
            first = _dot(a, w1_ref[0, g])
            second = _dot(a, w1_ref[1, g])
            second = pltpu.roll(second, ncp - 1, axis=0)
            bias = (_dot(pe_first, w1_ref[0, g]) + _dot(pe_second, w1_ref[1, g]))[0:1]
            hid = jax.nn.silu(first + second + bias).astype(BF16)
            outs.append(_dot(hid, w2_ref[...]))
        out_ref[0] = jnp.concatenate(outs, axis=1).astype(out_ref.dtype)

    one(k_ref, kpe_ref, kw1_ref, kw2_ref, ko_ref)
    one(v_ref, vpe_ref, vw1_ref, vw2_ref, vo_ref)


def _compress(k_tok, v_tok, kpe, kw1, kw2, vpe, vw1, vw2):
    B, ncp, width = k_tok.shape
    tok_spec = pl.BlockSpec((1, ncp, width), lambda b: (b, 0, 0))
    pe_spec = pl.BlockSpec(kpe.shape, lambda b: (0, 0))
    w1_spec = pl.BlockSpec(kw1.shape, lambda b: (0, 0, 0, 0))
    w2_spec = pl.BlockSpec(kw2.shape, lambda b: (0, 0))
    out_spec = pl.BlockSpec((1, ncp, KV_PAIR), lambda b: (b, 0, 0))
    out_shape = jax.ShapeDtypeStruct((B, ncp, KV_PAIR), BF16)
    return pl.pallas_call(
        _compress_kernel,
        grid=(B,),
        in_specs=[tok_spec, tok_spec, pe_spec, w1_spec, w2_spec, pe_spec, w1_spec, w2_spec],
        out_specs=[out_spec, out_spec],
        out_shape=[out_shape, out_shape],
        compiler_params=pltpu.CompilerParams(
            dimension_semantics=("parallel",), vmem_limit_bytes=VMEM_LIMIT),
        name="compress",
    )(k_tok, v_tok, kpe, kw1, kw2, vpe, vw1, vw2)


def _softmax_rows(s, mask):
    sm = jnp.where(mask, s, NEG_INF)
    m = jnp.max(sm, axis=-1, keepdims=True)
    e = jnp.exp(sm - m)
    p = e / jnp.sum(e, axis=-1, keepdims=True)
    return jnp.where(mask, p, 0.0)


def _nsa_kernel(q_ref, kc_ref, vc_ref, ks_ref, vs_ref, kw_ref, vw_ref, gates_ref,
                ovl_ref, exp_ref, o_ref, score_ref, *, n_sel):
    R = NSA_GROUP_SIZE
    QB = Q_BLOCK
    ncp = kc_ref.shape[1]
    ns = ovl_ref.shape[1]
    c = pl.program_id(1)
    start = c * QB
    t_col = start + lax.broadcasted_iota(jnp.int32, (QB, 1), 0)
    gates = gates_ref[0]
    lane = lax.broadcasted_iota(jnp.int32, (1, LANES), 1)

    kc = kc_ref[0]
    vc = vc_ref[0]
    cmp_end = lax.broadcasted_iota(jnp.int32, (1, ncp), 1) * CMP_STRIDE + (CMP_BLOCK - 1)
    mask_c = (cmp_end <= t_col)[None]

    sel_j = lax.broadcasted_iota(jnp.int32, (1, ns), 1)
    cur = t_col // SEL_BLOCK
    valid = sel_j <= cur
    forced = (sel_j == 0) | (sel_j == cur) | (sel_j == cur - 1)
    j_row = lax.broadcasted_iota(jnp.int32, (ns, QB), 0)
    n_valid_blocks = (start + QB) // SEL_BLOCK

    win_start = pl.multiple_of(jnp.maximum(start - WINDOW, 0), QB)
    wpos = win_start + lax.broadcasted_iota(jnp.int32, (1, WIN_KEYS), 1)
    mask_w = ((wpos <= t_col) & (wpos > t_col - WINDOW))[None]

    n_chunks = (start + QB + SEL_KEY_CHUNK - 1) // SEL_KEY_CHUNK

    head_out = [None] * NSA_HEADS
    for g in range(NSA_KV_GROUPS):
        q4 = q_ref[0, g * R:(g + 1) * R].reshape(R * QB, LANES)

        s_c = _dot_nt(q4, kc).reshape(R, QB, ncp)
        p_c = _softmax_rows(s_c, mask_c)
        o_cmp = _dot(p_c.reshape(R * QB, ncp).astype(BF16), vc)

        imp = _dot(jnp.sum(p_c, axis=0), ovl_ref[...])
        score = jnp.where(valid, imp + jnp.where(forced, FORCE_BONUS, 0.0), NEG_INF)
        score_t = score.T
        score_ref[...] = score_t

        def rank_body(jp, cnt):
            other = score_ref[pl.ds(jp, 1), :]
            ahead = (other > score_t) | ((other == score_t) & (jp < j_row))
            return cnt + jnp.where(ahead, 1.0, 0.0)

        rank = lax.fori_loop(0, n_valid_blocks, rank_body, jnp.zeros((ns, QB), F32))
        sel_t = jnp.where((rank < n_sel) & (score_t > 0.5 * NEG_INF), 1.0, 0.0)
        sel = sel_t.T.astype(BF16)

        def sel_body(kb, carry):
            m_run, l_run, acc = carry
            k0 = pl.multiple_of(kb * SEL_KEY_CHUNK, SEL_KEY_CHUNK)
            k = ks_ref[0, pl.ds(k0, SEL_KEY_CHUNK), :]
            v = vs_ref[0, pl.ds(k0, SEL_KEY_CHUNK), :]
            s = _dot_nt(q4, k).reshape(R, QB, SEL_KEY_CHUNK)
            picked = _dot(sel, exp_ref[:, pl.ds(k0, SEL_KEY_CHUNK)])
            kpos = k0 + lax.broadcasted_iota(jnp.int32, (1, SEL_KEY_CHUNK), 1)
            mask = (jnp.where(kpos <= t_col, picked, 0.0) > 0.5)[None]
            sm = jnp.where(mask, s, NEG_INF)
            m_new = jnp.maximum(m_run, jnp.max(sm, axis=-1, keepdims=True))
            alpha = jnp.exp(m_run - m_new)
            p = jnp.where(mask, jnp.exp(sm - m_new), 0.0)
            l_new = alpha * l_run + jnp.sum(p, axis=-1, keepdims=True)
            pv = _dot(p.reshape(R * QB, SEL_KEY_CHUNK).astype(BF16), v).reshape(R, QB, LANES)
            return m_new, l_new, alpha * acc + pv

        init = (jnp.full((R, QB, 1), NEG_INF, F32), jnp.zeros((R, QB, 1), F32),
                jnp.zeros((R, QB, LANES), F32))
        _, l_fin, acc = lax.fori_loop(0, n_chunks, sel_body, init)
        o_sel = acc / l_fin

        kw = kw_ref[0, pl.ds(win_start, WIN_KEYS), :]
        vw = vw_ref[0, pl.ds(win_start, WIN_KEYS), :]
        s_w = _dot_nt(q4, kw).reshape(R, QB, WIN_KEYS)
        p_w = _softmax_rows(s_w, mask_w)
        o_win = _dot(p_w.reshape(R * QB, WIN_KEYS).astype(BF16), vw).reshape(R, QB, LANES)

        o_cmp = o_cmp.reshape(R, QB, LANES)
        for r in range(R):
            h = g * R + r
            o = (gates[:, 3 * h:3 * h + 1] * o_cmp[r] + gates[:, 3 * h + 1:3 * h + 2] * o_sel[r]
                 + gates[:, 3 * h + 2:3 * h + 3] * o_win[r])
            if h % 2 != g:
                o = pltpu.roll(o, HEAD_DIM, axis=1)
            head_out[h] = o

    pairs = [jnp.where(lane < HEAD_DIM, head_out[2 * a], head_out[2 * a + 1])
             for a in range(NSA_HEADS // 2)]
    o_ref[0] = jnp.concatenate(pairs, axis=1).astype(o_ref.dtype)


def _nsa(q, kc, vc, ks, vs, kw, vw, gates, overlap, expand):
    B, H, S, _ = q.shape
    ncp = kc.shape[1]
    ns = overlap.shape[1]
    nq = S // Q_BLOCK
    whole = lambda n: pl.BlockSpec((1, n, KV_PAIR), lambda b, c: (b, 0, 0))
    return pl.pallas_call(
        functools.partial(_nsa_kernel, n_sel=min(SEL_TOP_N, ns)),
        grid=(B, nq),
        in_specs=[
            pl.BlockSpec((1, H, Q_BLOCK, LANES), lambda b, c: (b, 0, c, 0)),
            whole(ncp), whole(ncp), whole(S), whole(S), whole(S), whole(S),
            pl.BlockSpec((1, Q_BLOCK, LANES), lambda b, c: (b, c, 0)),
            pl.BlockSpec(overlap.shape, lambda b, c: (0, 0)),
            pl.BlockSpec(expand.shape, lambda b, c: (0, 0)),
        ],
        out_specs=pl.BlockSpec((1, Q_BLOCK, H * HEAD_DIM), lambda b, c: (b, c, 0)),
        out_shape=jax.ShapeDtypeStruct((B, S, H * HEAD_DIM), BF16),
        scratch_shapes=[pltpu.VMEM((ns, Q_BLOCK), F32)],
        compiler_params=pltpu.CompilerParams(
            dimension_semantics=("parallel", "arbitrary"), vmem_limit_bytes=VMEM_LIMIT),
        name="nsa",
    )(q, kc, vc, ks, vs, kw, vw, gates, overlap, expand)


def _merge_kernel(x_ref, nsa_ref, sgu_ref, gate_ref, wbn_ref, wbs_ref, wout_ref, g_ref, b_ref,
                  o_ref, *, alpha):
    d = x_ref.shape[1]
    a = _dot(nsa_ref[...], wbn_ref[...])
    s = _dot(sgu_ref[...], wbs_ref[...])
    merged = gate_ref[:, :d].astype(F32) * a + gate_ref[:, d:].astype(F32) * s
    mix = _dot(merged.astype(BF16), wout_ref[...])
    o_ref[...] = _layer_norm(alpha * x_ref[...] + mix, g_ref[...], b_ref[...])


def _merge(x2, o_nsa, o_sgu, merge_g, wbn, wbs, wout, ln_g, ln_b, alpha):
    T, D = x2.shape
    tm = TOKEN_TILE
    row = lambda w: pl.BlockSpec((tm, w), lambda i: (i, 0))
    full = lambda a: pl.BlockSpec(a.shape, lambda i: (0, 0))
    return pl.pallas_call(
        functools.partial(_merge_kernel, alpha=alpha),
        grid=(T // tm,),
        in_specs=[row(D), row(o_nsa.shape[1]), row(o_sgu.shape[1]), row(2 * D),
                  full(wbn), full(wbs), full(wout), full(ln_g), full(ln_b)],
        out_specs=row(D),
        out_shape=jax.ShapeDtypeStruct((T, D), F32),
        compiler_params=pltpu.CompilerParams(
            dimension_semantics=("parallel",), vmem_limit_bytes=VMEM_LIMIT),
        name="merge",
    )(x2, o_nsa, o_sgu, merge_g, wbn, wbs, wout, ln_g, ln_b)


def _ffn_kernel(h_ref, wg_ref, wu_ref, wd_ref, g_ref, b_ref, o_ref, hb_ref, acc_ref, *, alpha):
    j = pl.program_id(1)

    @pl.when(j == 0)
    def _():
        hb_ref[...] = h_ref[...].astype(BF16)
        acc_ref[...] = jnp.zeros_like(acc_ref)

    hb = hb_ref[...]
    act = jax.nn.silu(_dot(hb, wg_ref[...])) * _dot(hb, wu_ref[...])
    acc_ref[...] += _dot(act.astype(BF16), wd_ref[...])

    @pl.when(j == pl.num_programs(1) - 1)
    def _():
        o_ref[...] = _layer_norm(alpha * h_ref[...] + acc_ref[...], g_ref[...], b_ref[...])


def _ffn(h, wg, wu, wd, ln_g, ln_b, alpha):
    T, D = h.shape
    hidden = wg.shape[1]
    tm = TOKEN_TILE
    n_h = 2
    th = hidden // n_h
    assert th * n_h == hidden and th % LANES == 0
    return pl.pallas_call(
        functools.partial(_ffn_kernel, alpha=alpha),
        grid=(T // tm, n_h),
        in_specs=[
            pl.BlockSpec((tm, D), lambda i, j: (i, 0)),
            pl.BlockSpec((D, th), lambda i, j: (0, j)),
            pl.BlockSpec((D, th), lambda i, j: (0, j)),
            pl.BlockSpec((th, D), lambda i, j: (j, 0)),
            pl.BlockSpec((1, D), lambda i, j: (0, 0)),
            pl.BlockSpec((1, D), lambda i, j: (0, 0)),
        ],
        out_specs=pl.BlockSpec((tm, D), lambda i, j: (i, 0)),
        out_shape=jax.ShapeDtypeStruct((T, D), F32),
        scratch_shapes=[pltpu.VMEM((tm, D), BF16), pltpu.VMEM((tm, D), F32)],
        compiler_params=pltpu.CompilerParams(
            dimension_semantics=("parallel", "arbitrary"), vmem_limit_bytes=VMEM_LIMIT),
        name="ffn",
    )(h, wg, wu, wd, ln_g, ln_b)


def _split_sizes(d_model):
    return [Q_WIDTH] + [KV_PAIR] * 6 + [3 * NSA_HEADS, SGU_WIDTH, SGU_WIDTH, 2 * d_model]


def _compress_weights(w1):
    hid = w1.shape[1]
    w = w1.reshape(2, CMP_STRIDE, 1, HEAD_DIM, hid)
    eye = jnp.eye(NSA_KV_GROUPS, dtype=w1.dtype)
    wg = w[:, None] * eye[None, :, None, :, None, None]
    return wg.reshape(2, NSA_KV_GROUPS, CMP_STRIDE * KV_PAIR, hid).astype(BF16)


def _compress_pe(pe):
    p = jnp.broadcast_to(pe.reshape(2, CMP_STRIDE, 1, HEAD_DIM), (2, CMP_STRIDE, NSA_KV_GROUPS, HEAD_DIM))
    return p.reshape(2, CMP_STRIDE * KV_PAIR)


def kernel(x, positions, w_in, pe_ck, w_ck1, w_ck2, pe_cv, w_cv1, w_cv2, ln_sgu_g, ln_sgu_b,
           w_spatial, b_spatial, w_branch_nsa, w_branch_sgu, w_out, ln1_g, ln1_b,
           w_ffn_gate, w_ffn_up, w_ffn_down, ln2_g, ln2_b):
    B, S, D = x.shape
    depth = w_in.shape[0]
    alpha = (2.0 * depth) ** 0.25
    assert S % TOKEN_TILE == 0 and S % SEL_KEY_CHUNK == 0 and S >= WIN_KEYS
    ncp = S // CMP_STRIDE
    ns = S // SEL_BLOCK

    freqs = ROPE_THETA ** (-jnp.arange(ROPE_HALF, dtype=F32) / ROPE_HALF)
    d_idx = np.arange(LANES) % HEAD_DIM
    freq_lane = jnp.where(d_idx < ROPE_DIM, freqs[d_idx % ROPE_HALF], 0.0).reshape(1, LANES)
    pos_f = positions.astype(F32).reshape(B * S, 1)

    cs = np.arange(ncp)[:, None] * CMP_STRIDE
    ss = np.arange(ns)[None, :] * SEL_BLOCK
    overlap = np.clip(np.minimum(cs + CMP_BLOCK, ss + SEL_BLOCK) - np.maximum(cs, ss), 0, None)
    overlap = jnp.asarray(overlap.astype(np.float32) / CMP_BLOCK)
    expand = jnp.asarray(np.arange(ns)[:, None] == (np.arange(S)[None, :] // SEL_BLOCK), dtype=BF16)

    sizes = _split_sizes(D)
    offs = np.concatenate([[0], np.cumsum(sizes)])
    seg = lambda w, i: w[:, offs[i]:offs[i + 1]]

    h = x.reshape(B * S, D)
    for l in range(depth):
        w = w_in[l]
        gate_cols = jnp.pad(seg(w, 7), ((0, 0), (0, LANES - sizes[7])))
        w_all = jnp.concatenate([seg(w, i) for i in (0, 1, 2, 3, 4, 5, 6, 8, 9, 10)] + [gate_cols],
                                axis=1).astype(BF16)
        bsp = jnp.repeat(b_spatial[l].T, SGU_WIDTH // SGU_GROUPS, axis=1)
        q, k_c, v_c, k_s, v_s, k_w, v_w, gates, o_sgu, merge_g = _inproj(
            h.reshape(B, S, D), pos_f, freq_lane, w_all,
            ln_sgu_g[l].reshape(1, -1), ln_sgu_b[l].reshape(1, -1), w_spatial[l], bsp)

        kc, vc = _compress(
            k_c.reshape(B, ncp, CMP_STRIDE * KV_PAIR), v_c.reshape(B, ncp, CMP_STRIDE * KV_PAIR),
            _compress_pe(pe_ck[l]), _compress_weights(w_ck1[l]), w_ck2[l].astype(BF16),
            _compress_pe(pe_cv[l]), _compress_weights(w_cv1[l]), w_cv2[l].astype(BF16))

        o_nsa = _nsa(q, kc, vc, k_s, v_s, k_w, v_w, gates, overlap, expand)

        h = _merge(h, o_nsa.reshape(B * S, -1), o_sgu, merge_g,
                   w_branch_nsa[l].astype(BF16), w_branch_sgu[l].astype(BF16), w_out[l].astype(BF16),
                   ln1_g[l].reshape(1, -1), ln1_b[l].reshape(1, -1), alpha)
        h = _ffn(h, w_ffn_gate[l].astype(BF16), w_ffn_up[l].astype(BF16), w_ffn_down[l].astype(BF16),
                 ln2_g[l].reshape(1, -1), ln2_b[l].reshape(1, -1), alpha)
    return h.reshape(B, S, D)
```

```python
import functools

import numpy as np
import jax
import jax.numpy as jnp
from jax import lax
from jax.experimental import pallas as pl
from jax.experimental.pallas import tpu as pltpu

F32 = jnp.float32
BF16 = jnp.bfloat16

HEAD_DIM = 64
NSA_HEADS = 8
NSA_KV_GROUPS = 2
NSA_GROUP_SIZE = NSA_HEADS // NSA_KV_GROUPS
CMP_BLOCK = 32
CMP_STRIDE = 16
CMP_HIDDEN = 128
SEL_BLOCK = 64
SEL_TOP_N = 16
WINDOW = 512
Q_BLOCK = 128
FORCE_BONUS = 1.0e4
ROPE_THETA = 500000.0
ROPE_DIM = HEAD_DIM // 4
ROPE_HALF = ROPE_DIM // 2
SGU_GROUPS = 4
SGU_CHUNK = 128
SGU_WIDTH = 512
LN_EPS = 1e-5
NEG_INF = -1e30

LANES = 128
KV_PAIR = NSA_KV_GROUPS * HEAD_DIM
assert KV_PAIR == LANES

TOKEN_TILE = 512
SEL_KEY_CHUNK = 1024
WIN_KEYS = WINDOW + Q_BLOCK
VMEM_LIMIT = 56 * 1024 * 1024


def _layer_norm(x, g, b):
    mu = jnp.mean(x, axis=-1, keepdims=True)
    xc = x - mu
    var = jnp.mean(xc * xc, axis=-1, keepdims=True)
    return xc * lax.rsqrt(var + LN_EPS) * g + b


def _dot(a, b):
    return jnp.dot(a, b, preferred_element_type=F32)


def _dot_nt(a, b):
    return lax.dot_general(a, b, (((1,), (1,)), ((), ())), preferred_element_type=F32)


Q_WIDTH = NSA_HEADS * HEAD_DIM
COL_Q = 0
COL_KV = COL_Q + Q_WIDTH
COL_U = COL_KV + 6 * KV_PAIR
COL_V = COL_U + SGU_WIDTH
COL_MERGE = COL_V + SGU_WIDTH
ROPED_KV = (0, 2, 4)
ONES_PADDED_V = (3, 5)
LOG2_E = 1.4426950408889634


def _inproj_kernel(x_ref, pos_ref, freq_ref, w_ref, lng_ref, lnb_ref, wsp_ref, bsp_ref,
                   q_ref, kc_ref, vc_ref, ks_ref, vs_ref, kw_ref, vw_ref,
                   gates_ref, sgu_ref, merge_ref, *, d_model):
    tm = x_ref.shape[0]
    col_gates = COL_MERGE + 2 * d_model
    xb = x_ref[...].astype(BF16)

    ang = pos_ref[...] * freq_ref[...]
    cos_t = jnp.cos(ang)
    sin_t = jnp.sin(ang)
    lane = lax.broadcasted_iota(jnp.int32, (1, LANES), 1)
    d_idx = lane % HEAD_DIM
    sin_lo = jnp.where(d_idx < ROPE_HALF, -sin_t, 0.0)
    sin_hi = jnp.where((d_idx >= ROPE_HALF) & (d_idx < ROPE_DIM), sin_t, 0.0)

    def rope(t):
        return (t * cos_t + pltpu.roll(t, LANES - ROPE_HALF, axis=1) * sin_lo
                + pltpu.roll(t, ROPE_HALF, axis=1) * sin_hi)

    rq = _dot(xb, w_ref[:, COL_Q:COL_Q + Q_WIDTH])
    scale = HEAD_DIM ** -0.5 * LOG2_E
    for pair in range(NSA_HEADS // 2):
        t = rope(rq[:, pair * LANES:(pair + 1) * LANES]) * scale
        t_sw = pltpu.roll(t, HEAD_DIM, axis=1)
        for half in range(2):
            h = 2 * pair + half
            grp = h // NSA_GROUP_SIZE
            src = t if half == grp else t_sw
            keep = (lane >= HEAD_DIM) if grp == 1 else (lane < HEAD_DIM)
            q_ref[0, h] = jnp.where(keep, src, 0.0).astype(q_ref.dtype)

    rkv = _dot(xb, w_ref[:, COL_KV:COL_KV + 6 * KV_PAIR])
    kv_refs = (kc_ref, vc_ref, ks_ref, vs_ref, kw_ref, vw_ref)
    for i, ref in enumerate(kv_refs):
        t = rkv[:, i * KV_PAIR:(i + 1) * KV_PAIR]
        if i in ROPED_KV:
            t = rope(t)
        if i in ONES_PADDED_V:
            for g in range(NSA_KV_GROUPS):
                keep = (lane >= HEAD_DIM) if g == 1 else (lane < HEAD_DIM)
                ref[0, g] = jnp.where(keep, t, 1.0).astype(ref.dtype)
        else:
            ref[0] = t.astype(ref.dtype)

    u = jax.nn.gelu(_dot(xb, w_ref[:, COL_U:COL_U + SGU_WIDTH]))
    v = jax.nn.gelu(_dot(xb, w_ref[:, COL_V:COL_V + SGU_WIDTH]))
    v = _layer_norm(v, lng_ref[...], lnb_ref[...]).astype(BF16)
    row = lax.broadcasted_iota(jnp.int32, (SGU_CHUNK, SGU_CHUNK), 0)
    col = lax.broadcasted_iota(jnp.int32, (SGU_CHUNK, SGU_CHUNK), 1)
    gdim = SGU_WIDTH // SGU_GROUPS
    w_sp = [jnp.where(col <= row, wsp_ref[g], 0.0).astype(BF16) for g in range(SGU_GROUPS)]
    for n in range(tm // SGU_CHUNK):
        rows = slice(n * SGU_CHUNK, (n + 1) * SGU_CHUNK)
        mixed = jnp.concatenate(
            [_dot(w_sp[g], v[rows, g * gdim:(g + 1) * gdim]) for g in range(SGU_GROUPS)], axis=1)
        sgu_ref[rows, :] = (u[rows, :] * (mixed + bsp_ref[...])).astype(sgu_ref.dtype)

    rm = _dot(xb, w_ref[:, COL_MERGE:COL_MERGE + 2 * d_model])
    merge_ref[...] = jax.nn.sigmoid(rm).astype(merge_ref.dtype)
    rg = _dot(xb, w_ref[:, col_gates:col_gates + LANES])
    gates_ref[0] = jax.nn.sigmoid(rg)


def _inproj(x, pos_f, freq_lane, w_all, ln_g, ln_b, w_sp, b_sp):
    B, S, D = x.shape
    tm = TOKEN_TILE
    nt = S // tm
    wcols = w_all.shape[1]
    x2 = x.reshape(B * S, D)
    kv_shape = jax.ShapeDtypeStruct((B, S, KV_PAIR), BF16)
    kv_spec = pl.BlockSpec((1, tm, KV_PAIR), lambda b, i: (b, i, 0))
    vg_shape = jax.ShapeDtypeStruct((B, NSA_KV_GROUPS, S, KV_PAIR), BF16)
    vg_spec = pl.BlockSpec((1, NSA_KV_GROUPS, tm, KV_PAIR), lambda b, i: (b, 0, i, 0))
    const2 = lambda b, i: (0, 0)
    return pl.pallas_call(
        functools.partial(_inproj_kernel, d_model=D),
        grid=(B, nt),
        in_specs=[
            pl.BlockSpec((tm, D), lambda b, i: (b * nt + i, 0)),
            pl.BlockSpec((tm, 1), lambda b, i: (b * nt + i, 0)),
            pl.BlockSpec((1, LANES), const2),
            pl.BlockSpec((D, wcols), const2),
            pl.BlockSpec((1, SGU_WIDTH), const2),
            pl.BlockSpec((1, SGU_WIDTH), const2),
            pl.BlockSpec((SGU_GROUPS, SGU_CHUNK, SGU_CHUNK), lambda b, i: (0, 0, 0)),
            pl.BlockSpec((SGU_CHUNK, SGU_WIDTH), const2),
        ],
        out_specs=[
            pl.BlockSpec((1, NSA_HEADS, tm, LANES), lambda b, i: (b, 0, i, 0)),
            kv_spec, kv_spec, kv_spec, vg_spec, kv_spec, vg_spec,
            pl.BlockSpec((1, tm, LANES), lambda b, i: (b, i, 0)),
            pl.BlockSpec((tm, SGU_WIDTH), lambda b, i: (b * nt + i, 0)),
            pl.BlockSpec((tm, 2 * D), lambda b, i: (b * nt + i, 0)),
        ],
        out_shape=[
            jax.ShapeDtypeStruct((B, NSA_HEADS, S, LANES), BF16),
            kv_shape, kv_shape, kv_shape, vg_shape, kv_shape, vg_shape,
            jax.ShapeDtypeStruct((B, S, LANES), F32),
            jax.ShapeDtypeStruct((B * S, SGU_WIDTH), BF16),
            jax.ShapeDtypeStruct((B * S, 2 * D), BF16),
        ],
        compiler_params=pltpu.CompilerParams(
            dimension_semantics=("parallel", "parallel"), vmem_limit_bytes=VMEM_LIMIT),
        name="inproj",
    )(x2, pos_f, freq_lane, w_all, ln_g, ln_b, w_sp, b_sp)


def _compress_kernel(k_ref, v_ref, kpe_ref, kw1_ref, kw2_ref, vpe_ref, vw1_ref, vw2_ref,
                     ko_ref, vo_ref):
    ncp = k_ref.shape[1]

    def one(tok_ref, pe_ref, w1_ref, w2_ref, out_ref):
        a = tok_ref[0]
        width = pe_ref.shape[1]
        pe_first = jnp.broadcast_to(pe_ref[0:1, :], (8, width)).astype(BF16)
        pe_second = jnp.broadcast_to(pe_ref[1:2, :], (8, width)).astype(BF16)
        outs = []
        for g in range(NSA_KV_GROUPS):
            first = _dot(a, w1_ref[0, g])
            second = _dot(a, w1_ref[1, g])
            second = pltpu.roll(second, ncp - 1, axis=0)
            bias = (_dot(pe_first, w1_ref[0, g]) + _dot(pe_second, w1_ref[1, g]))[0:1]
            hid = jax.nn.silu(first + second + bias).astype(BF16)
            outs.append(_dot(hid, w2_ref[...]))
        out_ref[0] = jnp.concatenate(outs, axis=1).astype(out_ref.dtype)

    one(k_ref, kpe_ref, kw1_ref, kw2_ref, ko_ref)
    one(v_ref, vpe_ref, vw1_ref, vw2_ref, vo_ref)


def _compress(k_tok, v_tok, kpe, kw1, kw2, vpe, vw1, vw2):
    B, ncp, width = k_tok.shape
    tok_spec = pl.BlockSpec((1, ncp, width), lambda b: (b, 0, 0))
    pe_spec = pl.BlockSpec(kpe.shape, lambda b: (0, 0))
    w1_spec = pl.BlockSpec(kw1.shape, lambda b: (0, 0, 0, 0))
    w2_spec = pl.BlockSpec(kw2.shape, lambda b: (0, 0))
    out_spec = pl.BlockSpec((1, ncp, KV_PAIR), lambda b: (b, 0, 0))
    out_shape = jax.ShapeDtypeStruct((B, ncp, KV_PAIR), BF16)
    return pl.pallas_call(
        _compress_kernel,
        grid=(B,),
        in_specs=[tok_spec, tok_spec, pe_spec, w1_spec, w2_spec, pe_spec, w1_spec, w2_spec],
        out_specs=[out_spec, out_spec],
        out_shape=[out_shape, out_shape],
        compiler_params=pltpu.CompilerParams(
            dimension_semantics=("parallel",), vmem_limit_bytes=VMEM_LIMIT),
        name="compress",
    )(k_tok, v_tok, kpe, kw1, kw2, vpe, vw1, vw2)


def _softmax_rows(s, mask):
    sm = jnp.where(mask, s, NEG_INF)
    m = jnp.max(sm, axis=-1, keepdims=True)
    e = jnp.exp2(sm - m)
    p = e * (1.0 / jnp.sum(e, axis=-1, keepdims=True))
    return jnp.where(mask, p, 0.0)


def _nsa_kernel(q_ref, kc_ref, vc_ref, ks_ref, vs_ref, kw_ref, vw_ref, gates_ref,
                ovl_ref, onehot_ref, o_ref, score_ref, lhs_ref, *, n_sel):
    R = NSA_GROUP_SIZE
    QB = Q_BLOCK
    ncp = kc_ref.shape[1]
    ns = ovl_ref.shape[1]
    c = pl.program_id(1)
    start = c * QB
    t_col = start + lax.broadcasted_iota(jnp.int32, (QB, 1), 0)
    gates = gates_ref[0]
    lane = lax.broadcasted_iota(jnp.int32, (1, LANES), 1)

    kc = kc_ref[0]
    vc = vc_ref[0]
    cmp_end = lax.broadcasted_iota(jnp.int32, (1, ncp), 1) * CMP_STRIDE + (CMP_BLOCK - 1)
    mask_c = (cmp_end <= t_col)[None]

    sel_j = lax.broadcasted_iota(jnp.int32, (1, ns), 1)
    cur = t_col // SEL_BLOCK
    valid = sel_j <= cur
    forced = (sel_j == 0) | (sel_j == cur) | (sel_j == cur - 1)
    j_row = lax.broadcasted_iota(jnp.int32, (ns, QB), 0)
    n_valid_blocks = (start + QB) // SEL_BLOCK

    win_start = pl.multiple_of(jnp.maximum(start - WINDOW, 0), QB)
    wpos = win_start + lax.broadcasted_iota(jnp.int32, (1, WIN_KEYS), 1)
    mask_w = ((wpos <= t_col) & (wpos > t_col - WINDOW))[None]

    first_own_block = start // SEL_BLOCK
    n_chunks = (start + SEL_KEY_CHUNK - 1) // SEL_KEY_CHUNK
    tri = (lax.broadcasted_iota(jnp.int32, (1, QB, QB), 2)
           <= lax.broadcasted_iota(jnp.int32, (1, QB, QB), 1))

    head_out = [None] * NSA_HEADS
    groups = range(NSA_KV_GROUPS)
    q4 = [q_ref[0, g * R:(g + 1) * R].reshape(R * QB, LANES) for g in groups]

    o_cmp, score_t = [], []
    for g in groups:
        s_c = _dot_nt(q4[g], kc).reshape(R, QB, ncp)
        p_c = _softmax_rows(s_c, mask_c)
        o_cmp.append(_dot(p_c.reshape(R * QB, ncp).astype(BF16), vc).reshape(R, QB, LANES))
        imp = _dot(jnp.sum(p_c, axis=0), ovl_ref[...])
        score = jnp.where(valid, imp + jnp.where(forced, FORCE_BONUS, 0.0), NEG_INF)
        score_t.append(score.T)
        score_ref[g] = score_t[g]

    def rank_body(jp, cnt):
        out = []
        for g in groups:
            other = score_ref[g, pl.ds(jp, 1), :]
            ahead = (other > score_t[g]) | ((other == score_t[g]) & (jp < j_row))
            out.append(cnt[g] + jnp.where(ahead, 1.0, 0.0))
        return tuple(out)

    rank = lax.fori_loop(0, n_valid_blocks, rank_body,
                         tuple(jnp.zeros((ns, QB), F32) for _ in groups))

    def _normalize(acc):
        return (acc * pltpu.roll(1.0 / acc, HEAD_DIM, axis=1)).reshape(R, QB, LANES)

    k_d = ks_ref[0, pl.ds(start, QB), :]
    carry = []
    for g in groups:
        picked = (rank[g] < n_sel) & (score_t[g] > 0.5 * NEG_INF)
        bias = jnp.where(picked & (j_row < first_own_block), 0.0, NEG_INF).T.astype(BF16)
        lhs_ref[g, :, :LANES] = q4[g]
        lhs_ref[g, :, LANES:] = jnp.concatenate([bias] * R, axis=0)
        s_d = jnp.where(tri, _dot_nt(q4[g], k_d).reshape(R, QB, QB), NEG_INF).reshape(R * QB, QB)
        m_run = jnp.max(s_d, axis=-1, keepdims=True)
        p_d = jnp.exp2(s_d - m_run)
        carry += [m_run, _dot(p_d.astype(BF16), vs_ref[0, g, pl.ds(start, QB), :])]

    def sel_body(kb, carry):
        k0 = pl.multiple_of(kb * SEL_KEY_CHUNK, SEL_KEY_CHUNK)
        ke = jnp.concatenate([ks_ref[0, pl.ds(k0, SEL_KEY_CHUNK), :],
                              onehot_ref[pl.ds(k0, SEL_KEY_CHUNK), :]], axis=1)
        out = []
        for g in groups:
            m_run, acc = carry[2 * g:2 * g + 2]
            s = _dot_nt(lhs_ref[g], ke)
            m_new = jnp.maximum(m_run, jnp.max(s, axis=-1, keepdims=True))
            p = jnp.exp2(s - m_new).astype(BF16)
            pv = _dot(p, vs_ref[0, g, pl.ds(k0, SEL_KEY_CHUNK), :])
            out += [m_new, jnp.exp2(m_run - m_new) * acc + pv]
        return tuple(out)

    carry = lax.fori_loop(0, n_chunks, sel_body, tuple(carry))

    kw = kw_ref[0, pl.ds(win_start, WIN_KEYS), :]
    for g in groups:
        o_sel = _normalize(carry[2 * g + 1])
        s_w = jnp.where(mask_w, _dot_nt(q4[g], kw).reshape(R, QB, WIN_KEYS), NEG_INF)
        s_w = s_w.reshape(R * QB, WIN_KEYS)
        e_w = jnp.exp2(s_w - jnp.max(s_w, axis=-1, keepdims=True))
        o_win = _normalize(_dot(e_w.astype(BF16), vw_ref[0, g, pl.ds(win_start, WIN_KEYS), :]))
        for r in range(R):
            h = g * R + r
            o = (gates[:, 3 * h:3 * h + 1] * o_cmp[g][r] + gates[:, 3 * h + 1:3 * h + 2] * o_sel[r]
                 + gates[:, 3 * h + 2:3 * h + 3] * o_win[r])
            if h % 2 != g:
                o = pltpu.roll(o, HEAD_DIM, axis=1)
            head_out[h] = o

    pairs = [jnp.where(lane < HEAD_DIM, head_out[2 * a], head_out[2 * a + 1])
             for a in range(NSA_HEADS // 2)]
    o_ref[0] = jnp.concatenate(pairs, axis=1).astype(o_ref.dtype)


def _nsa(q, kc, vc, ks, vs, kw, vw, gates, overlap, onehot):
    B, H, S, _ = q.shape
    ncp = kc.shape[1]
    ns = overlap.shape[1]
    nq = S // Q_BLOCK
    whole = lambda n: pl.BlockSpec((1, n, KV_PAIR), lambda b, c: (b, 0, 0))
    whole_v = pl.BlockSpec((1, NSA_KV_GROUPS, S, KV_PAIR), lambda b, c: (b, 0, 0, 0))
    return pl.pallas_call(
        functools.partial(_nsa_kernel, n_sel=min(SEL_TOP_N, ns)),
        grid=(B, nq),
        in_specs=[
            pl.BlockSpec((1, H, Q_BLOCK, LANES), lambda b, c: (b, 0, c, 0)),
            whole(ncp), whole(ncp), whole(S), whole_v, whole(S), whole_v,
            pl.BlockSpec((1, Q_BLOCK, LANES), lambda b, c: (b, c, 0)),
            pl.BlockSpec(overlap.shape, lambda b, c: (0, 0)),
            pl.BlockSpec(onehot.shape, lambda b, c: (0, 0)),
        ],
        out_specs=pl.BlockSpec((1, Q_BLOCK, H * HEAD_DIM), lambda b, c: (b, c, 0)),
        out_shape=jax.ShapeDtypeStruct((B, S, H * HEAD_DIM), BF16),
        scratch_shapes=[pltpu.VMEM((NSA_KV_GROUPS, ns, Q_BLOCK), F32),
                        pltpu.VMEM((NSA_KV_GROUPS, NSA_GROUP_SIZE * Q_BLOCK, LANES + ns), BF16)],
        compiler_params=pltpu.CompilerParams(
            dimension_semantics=("parallel", "arbitrary"), vmem_limit_bytes=VMEM_LIMIT),
        name="nsa",
    )(q, kc, vc, ks, vs, kw, vw, gates, overlap, onehot)


def _merge_kernel(x_ref, nsa_ref, sgu_ref, gate_ref, wbn_ref, wbs_ref, wout_ref, g_ref, b_ref,
                  o_ref, *, alpha):
    d = x_ref.shape[1]
    a = _dot(nsa_ref[...], wbn_ref[...])
    s = _dot(sgu_ref[...], wbs_ref[...])
    merged = gate_ref[:, :d].astype(F32) * a + gate_ref[:, d:].astype(F32) * s
    mix = _dot(merged.astype(BF16), wout_ref[...])
    o_ref[...] = _layer_norm(alpha * x_ref[...] + mix, g_ref[...], b_ref[...])


def _merge(x2, o_nsa, o_sgu, merge_g, wbn, wbs, wout, ln_g, ln_b, alpha):
    T, D = x2.shape
    tm = TOKEN_TILE
    row = lambda w: pl.BlockSpec((tm, w), lambda i: (i, 0))
    full = lambda a: pl.BlockSpec(a.shape, lambda i: (0, 0))
    return pl.pallas_call(
        functools.partial(_merge_kernel, alpha=alpha),
        grid=(T // tm,),
        in_specs=[row(D), row(o_nsa.shape[1]), row(o_sgu.shape[1]), row(2 * D),
                  full(wbn), full(wbs), full(wout), full(ln_g), full(ln_b)],
        out_specs=row(D),
        out_shape=jax.ShapeDtypeStruct((T, D), F32),
        compiler_params=pltpu.CompilerParams(
            dimension_semantics=("parallel",), vmem_limit_bytes=VMEM_LIMIT),
        name="merge",
    )(x2, o_nsa, o_sgu, merge_g, wbn, wbs, wout, ln_g, ln_b)


def _ffn_kernel(h_ref, wg_ref, wu_ref, wd_ref, g_ref, b_ref, o_ref, hb_ref, acc_ref, *, alpha):
    j = pl.program_id(1)

    @pl.when(j == 0)
    def _():
        hb_ref[...] = h_ref[...].astype(BF16)
        acc_ref[...] = jnp.zeros_like(acc_ref)

    hb = hb_ref[...]
    act = jax.nn.silu(_dot(hb, wg_ref[...])) * _dot(hb, wu_ref[...])
    acc_ref[...] += _dot(act.astype(BF16), wd_ref[...])

    @pl.when(j == pl.num_programs(1) - 1)
    def _():
        o_ref[...] = _layer_norm(alpha * h_ref[...] + acc_ref[...], g_ref[...], b_ref[...])


def _ffn(h, wg, wu, wd, ln_g, ln_b, alpha):
    T, D = h.shape
    hidden = wg.shape[1]
    tm = TOKEN_TILE
    n_h = 2
    th = hidden // n_h
    assert th * n_h == hidden and th % LANES == 0
    return pl.pallas_call(
        functools.partial(_ffn_kernel, alpha=alpha),
        grid=(T // tm, n_h),
        in_specs=[
            pl.BlockSpec((tm, D), lambda i, j: (i, 0)),
            pl.BlockSpec((D, th), lambda i, j: (0, j)),
            pl.BlockSpec((D, th), lambda i, j: (0, j)),
            pl.BlockSpec((th, D), lambda i, j: (j, 0)),
            pl.BlockSpec((1, D), lambda i, j: (0, 0)),
            pl.BlockSpec((1, D), lambda i, j: (0, 0)),
        ],
        out_specs=pl.BlockSpec((tm, D), lambda i, j: (i, 0)),
        out_shape=jax.ShapeDtypeStruct((T, D), F32),
        scratch_shapes=[pltpu.VMEM((tm, D), BF16), pltpu.VMEM((tm, D), F32)],
        compiler_params=pltpu.CompilerParams(
            dimension_semantics=("parallel", "arbitrary"), vmem_limit_bytes=VMEM_LIMIT),
        name="ffn",
    )(h, wg, wu, wd, ln_g, ln_b)


def _split_sizes(d_model):
    return [Q_WIDTH] + [KV_PAIR] * 6 + [3 * NSA_HEADS, SGU_WIDTH, SGU_WIDTH, 2 * d_model]


def _compress_weights(w1):
    hid = w1.shape[1]
    w = w1.reshape(2, CMP_STRIDE, 1, HEAD_DIM, hid)
    eye = jnp.eye(NSA_KV_GROUPS, dtype=w1.dtype)
    wg = w[:, None] * eye[None, :, None, :, None, None]
    return wg.reshape(2, NSA_KV_GROUPS, CMP_STRIDE * KV_PAIR, hid).astype(BF16)


def _compress_pe(pe):
    p = jnp.broadcast_to(pe.reshape(2, CMP_STRIDE, 1, HEAD_DIM), (2, CMP_STRIDE, NSA_KV_GROUPS, HEAD_DIM))
    return p.reshape(2, CMP_STRIDE * KV_PAIR)


def kernel(x, positions, w_in, pe_ck, w_ck1, w_ck2, pe_cv, w_cv1, w_cv2, ln_sgu_g, ln_sgu_b,
           w_spatial, b_spatial, w_branch_nsa, w_branch_sgu, w_out, ln1_g, ln1_b,
           w_ffn_gate, w_ffn_up, w_ffn_down, ln2_g, ln2_b):
    B, S, D = x.shape
    depth = w_in.shape[0]
    alpha = (2.0 * depth) ** 0.25
    assert S % TOKEN_TILE == 0 and S % SEL_KEY_CHUNK == 0 and S >= WIN_KEYS
    ncp = S // CMP_STRIDE
    ns = S // SEL_BLOCK

    freqs = ROPE_THETA ** (-jnp.arange(ROPE_HALF, dtype=F32) / ROPE_HALF)
    d_idx = np.arange(LANES) % HEAD_DIM
    freq_lane = jnp.where(d_idx < ROPE_DIM, freqs[d_idx % ROPE_HALF], 0.0).reshape(1, LANES)
    pos_f = positions.astype(F32).reshape(B * S, 1)

    cs = np.arange(ncp)[:, None] * CMP_STRIDE
    ss = np.arange(ns)[None, :] * SEL_BLOCK
    overlap = np.clip(np.minimum(cs + CMP_BLOCK, ss + SEL_BLOCK) - np.maximum(cs, ss), 0, None)
    overlap = jnp.asarray(overlap.astype(np.float32) / CMP_BLOCK)
    onehot = jnp.asarray((np.arange(S)[:, None] // SEL_BLOCK) == np.arange(ns)[None, :], dtype=BF16)

    sizes = _split_sizes(D)
    offs = np.concatenate([[0], np.cumsum(sizes)])
    seg = lambda w, i: w[:, offs[i]:offs[i + 1]]

    h = x.reshape(B * S, D)
    for l in range(depth):
        w = w_in[l]
        gate_cols = jnp.pad(seg(w, 7), ((0, 0), (0, LANES - sizes[7])))
        w_all = jnp.concatenate([seg(w, i) for i in (0, 1, 2, 3, 4, 5, 6, 8, 9, 10)] + [gate_cols],
                                axis=1).astype(BF16)
        bsp = jnp.repeat(b_spatial[l].T, SGU_WIDTH // SGU_GROUPS, axis=1)
        q, k_c, v_c, k_s, v_s, k_w, v_w, gates, o_sgu, merge_g = _inproj(
            h.reshape(B, S, D), pos_f, freq_lane, w_all,
            ln_sgu_g[l].reshape(1, -1), ln_sgu_b[l].reshape(1, -1), w_spatial[l], bsp)

        kc, vc = _compress(
            k_c.reshape(B, ncp, CMP_STRIDE * KV_PAIR), v_c.reshape(B, ncp, CMP_STRIDE * KV_PAIR),
            _compress_pe(pe_ck[l]), _compress_weights(w_ck1[l]), w_ck2[l].astype(BF16),
            _compress_pe(pe_cv[l]), _compress_weights(w_cv1[l]), w_cv2[l].astype(BF16))

        o_nsa = _nsa(q, kc, vc, k_s, v_s, k_w, v_w, gates, overlap, onehot)

        h = _merge(h, o_nsa.reshape(B * S, -1), o_sgu, merge_g,
                   w_branch_nsa[l].astype(BF16), w_branch_sgu[l].astype(BF16), w_out[l].astype(BF16),
                   ln1_g[l].reshape(1, -1), ln1_b[l].reshape(1, -1), alpha)
        h = _ffn(h, w_ffn_gate[l].astype(BF16), w_ffn_up[l].astype(BF16), w_ffn_down[l].astype(BF16),
                 ln2_g[l].reshape(1, -1), ln2_b[l].reshape(1, -1), alpha)
    return h.reshape(B, S, D)
```

```python
import functools

import numpy as np
import jax
import jax.numpy as jnp
from jax import lax
from jax.experimental import pallas as pl
from jax.experimental.pallas import tpu as pltpu

F32 = jnp.float32
BF16 = jnp.bfloat16

HEAD_DIM = 64
NSA_HEADS = 8
NSA_KV_GROUPS = 2
NSA_GROUP_SIZE = NSA_HEADS // NSA_KV_GROUPS
CMP_BLOCK = 32
CMP_STRIDE = 16
CMP_HIDDEN = 128
SEL_BLOCK = 64
SEL_TOP_N = 16
WINDOW = 512
Q_BLOCK = 128
FORCE_BONUS = 1.0e4
N_FORCED = 3
assert FORCE_BONUS > NSA_GROUP_SIZE and SEL_TOP_N >= N_FORCED
ROPE_THETA = 500000.0
ROPE_DIM = HEAD_DIM // 4
ROPE_HALF = ROPE_DIM // 2
SGU_GROUPS = 4
SGU_CHUNK = 128
SGU_WIDTH = 512
LN_EPS = 1e-5
NEG_INF = -1e30

LANES = 128
KV_PAIR = NSA_KV_GROUPS * HEAD_DIM
assert KV_PAIR == LANES

TOKEN_TILE = 512
SEL_KEY_CHUNK = 1024
WIN_KEYS = WINDOW + Q_BLOCK
VMEM_LIMIT = 56 * 1024 * 1024


def _layer_norm(x, g, b):
    mu = jnp.mean(x, axis=-1, keepdims=True)
    xc = x - mu
    var = jnp.mean(xc * xc, axis=-1, keepdims=True)
    return xc * lax.rsqrt(var + LN_EPS) * g + b


def _dot(a, b):
    return jnp.dot(a, b, preferred_element_type=F32)


def _dot_nt(a, b):
    return lax.dot_general(a, b, (((1,), (1,)), ((), ())), preferred_element_type=F32)


Q_WIDTH = NSA_HEADS * HEAD_DIM
COL_Q = 0
COL_KV = COL_Q + Q_WIDTH
COL_U = COL_KV + 6 * KV_PAIR
COL_V = COL_U + SGU_WIDTH
COL_MERGE = COL_V + SGU_WIDTH
ROPED_KV = (0, 2, 4)
LOG2_E = 1.4426950408889634


def _inproj_kernel(x_ref, pos_ref, freq_ref, w_ref, lng_ref, lnb_ref, wsp_ref, bsp_ref,
                   q_ref, kc_ref, vc_ref, ks_ref, vs_ref, kw_ref, vw_ref,
                   gates_ref, sgu_ref, merge_ref, *, d_model):
    tm = x_ref.shape[0]
    col_gates = COL_MERGE + 2 * d_model
    xb = x_ref[...].astype(BF16)

    ang = freq_ref[...] * pos_ref[0]
    cos_f = jnp.cos(ang)
    sin_f = jnp.sin(ang)
    rest = HEAD_DIM - ROPE_DIM
    per_head = lambda parts: jnp.concatenate(parts * (LANES // HEAD_DIM), axis=0).T
    cos_t = per_head([cos_f, cos_f, jnp.ones((rest, tm), F32)])
    sin_lo = per_head([-sin_f, jnp.zeros((ROPE_HALF + rest, tm), F32)])
    sin_hi = per_head([jnp.zeros((ROPE_HALF, tm), F32), sin_f, jnp.zeros((rest, tm), F32)])
    lane = lax.broadcasted_iota(jnp.int32, (1, LANES), 1)

    def rope(t):
        return (t * cos_t + pltpu.roll(t, LANES - ROPE_HALF, axis=1) * sin_lo
                + pltpu.roll(t, ROPE_HALF, axis=1) * sin_hi)

    rq = _dot(xb, w_ref[:, COL_Q:COL_Q + Q_WIDTH])
    scale = HEAD_DIM ** -0.5 * LOG2_E
    for pair in range(NSA_HEADS // 2):
        t = rope(rq[:, pair * LANES:(pair + 1) * LANES]) * scale
        t_sw = pltpu.roll(t, HEAD_DIM, axis=1)
        for half in range(2):
            h = 2 * pair + half
            grp = h // NSA_GROUP_SIZE
            src = t if half == grp else t_sw
            keep = (lane >= HEAD_DIM) if grp == 1 else (lane < HEAD_DIM)
            q_ref[0, h] = jnp.where(keep, src, 0.0).astype(q_ref.dtype)

    rkv = _dot(xb, w_ref[:, COL_KV:COL_KV + 6 * KV_PAIR])
    kv_refs = (kc_ref, vc_ref, ks_ref, vs_ref, kw_ref, vw_ref)
    for i, ref in enumerate(kv_refs):
        t = rkv[:, i * KV_PAIR:(i + 1) * KV_PAIR]
        if i in ROPED_KV:
            t = rope(t)
        ref[0] = t.astype(ref.dtype)

    u = jax.nn.gelu(_dot(xb, w_ref[:, COL_U:COL_U + SGU_WIDTH]))
    v = jax.nn.gelu(_dot(xb, w_ref[:, COL_V:COL_V + SGU_WIDTH]))
    v = _layer_norm(v, lng_ref[...], lnb_ref[...]).astype(BF16)
    row = lax.broadcasted_iota(jnp.int32, (SGU_CHUNK, SGU_CHUNK), 0)
    col = lax.broadcasted_iota(jnp.int32, (SGU_CHUNK, SGU_CHUNK), 1)
    gdim = SGU_WIDTH // SGU_GROUPS
    w_sp = [jnp.where(col <= row, wsp_ref[g], 0.0).astype(BF16) for g in range(SGU_GROUPS)]
    for n in range(tm // SGU_CHUNK):
        rows = slice(n * SGU_CHUNK, (n + 1) * SGU_CHUNK)
        mixed = jnp.concatenate(
            [_dot(w_sp[g], v[rows, g * gdim:(g + 1) * gdim]) for g in range(SGU_GROUPS)], axis=1)
        sgu_ref[rows, :] = (u[rows, :] * (mixed + bsp_ref[...])).astype(sgu_ref.dtype)

    rm = _dot(xb, w_ref[:, COL_MERGE:COL_MERGE + 2 * d_model])
    merge_ref[...] = jax.nn.sigmoid(rm).astype(merge_ref.dtype)
    rg = _dot(xb, w_ref[:, col_gates:col_gates + LANES])
    gates_ref[0] = jax.nn.sigmoid(rg)


def _inproj(x, pos_rows, freqs, w_all, ln_g, ln_b, w_sp, b_sp):
    B, S, D = x.shape
    tm = TOKEN_TILE
    nt = S // tm
    wcols = w_all.shape[1]
    x2 = x.reshape(B * S, D)
    kv_shape = jax.ShapeDtypeStruct((B, S, KV_PAIR), BF16)
    kv_spec = pl.BlockSpec((1, tm, KV_PAIR), lambda b, i: (b, i, 0))
    const2 = lambda b, i: (0, 0)
    return pl.pallas_call(
        functools.partial(_inproj_kernel, d_model=D),
        grid=(B, nt),
        in_specs=[
            pl.BlockSpec((tm, D), lambda b, i: (b * nt + i, 0)),
            pl.BlockSpec((1, 1, tm), lambda b, i: (b * nt + i, 0, 0)),
            pl.BlockSpec((ROPE_HALF, 1), const2),
            pl.BlockSpec((D, wcols), const2),
            pl.BlockSpec((1, SGU_WIDTH), const2),
            pl.BlockSpec((1, SGU_WIDTH), const2),
            pl.BlockSpec((SGU_GROUPS, SGU_CHUNK, SGU_CHUNK), lambda b, i: (0, 0, 0)),
            pl.BlockSpec((SGU_CHUNK, SGU_WIDTH), const2),
        ],
        out_specs=[
            pl.BlockSpec((1, NSA_HEADS, tm, LANES), lambda b, i: (b, 0, i, 0)),
            kv_spec, kv_spec, kv_spec, kv_spec, kv_spec, kv_spec,
            pl.BlockSpec((1, tm, LANES), lambda b, i: (b, i, 0)),
            pl.BlockSpec((tm, SGU_WIDTH), lambda b, i: (b * nt + i, 0)),
            pl.BlockSpec((tm, 2 * D), lambda b, i: (b * nt + i, 0)),
        ],
        out_shape=[
            jax.ShapeDtypeStruct((B, NSA_HEADS, S, LANES), BF16),
            kv_shape, kv_shape, kv_shape, kv_shape, kv_shape, kv_shape,
            jax.ShapeDtypeStruct((B, S, LANES), F32),
            jax.ShapeDtypeStruct((B * S, SGU_WIDTH), BF16),
            jax.ShapeDtypeStruct((B * S, 2 * D), BF16),
        ],
        compiler_params=pltpu.CompilerParams(
            dimension_semantics=("parallel", "parallel"), vmem_limit_bytes=VMEM_LIMIT),
        name="inproj",
    )(x2, pos_rows, freqs, w_all, ln_g, ln_b, w_sp, b_sp)


def _compress_kernel(k_ref, v_ref, kpe_ref, kw1_ref, kw2_ref, vpe_ref, vw1_ref, vw2_ref,
                     ko_ref, vo_ref):
    ncp = k_ref.shape[1]

    def one(tok_ref, pe_ref, w1_ref, w2_ref, out_ref):
        a = tok_ref[0]
        width = pe_ref.shape[1]
        pe_first = jnp.broadcast_to(pe_ref[0:1, :], (8, width)).astype(BF16)
        pe_second = jnp.broadcast_to(pe_ref[1:2, :], (8, width)).astype(BF16)
        outs = []
        for g in range(NSA_KV_GROUPS):
            first = _dot(a, w1_ref[0, g])
            second = _dot(a, w1_ref[1, g])
            second = pltpu.roll(second, ncp - 1, axis=0)
            bias = (_dot(pe_first, w1_ref[0, g]) + _dot(pe_second, w1_ref[1, g]))[0:1]
            hid = jax.nn.silu(first + second + bias).astype(BF16)
            outs.append(_dot(hid, w2_ref[...]))
        out_ref[0] = jnp.concatenate(outs, axis=1).astype(out_ref.dtype)

    one(k_ref, kpe_ref, kw1_ref, kw2_ref, ko_ref)
    one(v_ref, vpe_ref, vw1_ref, vw2_ref, vo_ref)


def _compress(k_tok, v_tok, kpe, kw1, kw2, vpe, vw1, vw2):
    B, ncp, width = k_tok.shape
    tok_spec = pl.BlockSpec((1, ncp, width), lambda b: (b, 0, 0))
    pe_spec = pl.BlockSpec(kpe.shape, lambda b: (0, 0))
    w1_spec = pl.BlockSpec(kw1.shape, lambda b: (0, 0, 0, 0))
    w2_spec = pl.BlockSpec(kw2.shape, lambda b: (0, 0))
    out_spec = pl.BlockSpec((1, ncp, KV_PAIR), lambda b: (b, 0, 0))
    out_shape = jax.ShapeDtypeStruct((B, ncp, KV_PAIR), BF16)
    return pl.pallas_call(
        _compress_kernel,
        grid=(B,),
        in_specs=[tok_spec, tok_spec, pe_spec, w1_spec, w2_spec, pe_spec, w1_spec, w2_spec],
        out_specs=[out_spec, out_spec],
        out_shape=[out_shape, out_shape],
        compiler_params=pltpu.CompilerParams(
            dimension_semantics=("parallel",), vmem_limit_bytes=VMEM_LIMIT),
        name="compress",
    )(k_tok, v_tok, kpe, kw1, kw2, vpe, vw1, vw2)


def _softmax_rows(s, mask, row_valid):
    sm = jnp.where(mask, s, NEG_INF)
    m = jnp.max(sm, axis=-1, keepdims=True)
    e = jnp.exp2(sm - m)
    return e * jnp.where(row_valid, 1.0 / jnp.sum(e, axis=-1, keepdims=True), 0.0)


KNOCKED_OUT = -(2.0 ** 127)


def _top_rows(s, row_idx, rounds):
    n = float(s.shape[0])
    row_idx = row_idx.astype(F32)
    for _ in range(rounds):
        m = jnp.max(s, axis=0, keepdims=True)
        first = jnp.min(jnp.where(s == m, row_idx, n), axis=0, keepdims=True)
        s = jnp.where(row_idx == first, KNOCKED_OUT, s)
    return s <= KNOCKED_OUT


def _nsa_kernel(q_ref, kc_ref, vc_ref, ks_ref, vs_ref, kw_ref, vw_ref, gates_ref,
                ovl_ref, onehot_ref, gexp_ref, o_ref, lhs_ref, *, n_sel):
    R = NSA_GROUP_SIZE
    QB = Q_BLOCK
    ncp = kc_ref.shape[1]
    ns = ovl_ref.shape[1]
    c = pl.program_id(1)
    start = c * QB
    t_col = start + lax.broadcasted_iota(jnp.int32, (QB, 1), 0)
    lane = lax.broadcasted_iota(jnp.int32, (1, LANES), 1)

    kc = kc_ref[0]
    vc = vc_ref[0]
    cmp_end = lax.broadcasted_iota(jnp.int32, (1, ncp), 1) * CMP_STRIDE + (CMP_BLOCK - 1)
    mask_c = (cmp_end <= t_col)[None]

    row_valid = (t_col >= CMP_BLOCK - 1)[None]

    cur_row = (start + lax.broadcasted_iota(jnp.int32, (1, QB), 1)) // SEL_BLOCK
    j_row = lax.broadcasted_iota(jnp.int32, (ns, QB), 0)
    valid_t = j_row <= cur_row
    forced_t = (j_row == 0) | (j_row == cur_row) | (j_row == cur_row - 1)
    free_t = valid_t & jnp.logical_not(forced_t)

    win_start = pl.multiple_of(jnp.maximum(start - WINDOW, 0), QB)
    wpos = win_start + lax.broadcasted_iota(jnp.int32, (1, WIN_KEYS), 1)
    mask_w = ((wpos <= t_col) & (wpos > t_col - WINDOW))[None]

    first_own_block = start // SEL_BLOCK
    n_chunks = (start + SEL_KEY_CHUNK - 1) // SEL_KEY_CHUNK
    tri = (lax.broadcasted_iota(jnp.int32, (1, QB, QB), 2)
           <= lax.broadcasted_iota(jnp.int32, (1, QB, QB), 1))

    H = NSA_HEADS
    q_all = q_ref[0].reshape(H * QB, LANES)
    ones = jnp.ones((WIN_KEYS, LANES), BF16)
    low = lane < HEAD_DIM

    def with_ones(v):
        return jnp.concatenate([v, ones[:v.shape[0]]], axis=1)

    s_c = _dot_nt(q_all, kc).reshape(H, QB, ncp)
    p_c = _softmax_rows(s_c, mask_c, row_valid)
    o_cmp = _dot(p_c.reshape(H * QB, ncp).astype(BF16), vc).reshape(H, QB, LANES)
    bias = []
    for g in range(NSA_KV_GROUPS):
        imp_t = _dot(jnp.sum(p_c[g * R:(g + 1) * R], axis=0), ovl_ref[...]).T
        picked_t = forced_t | (_top_rows(jnp.where(free_t, imp_t, NEG_INF), j_row,
                                         n_sel - N_FORCED) & valid_t)
        bias_g = jnp.where(picked_t & (j_row < first_own_block), 0.0, NEG_INF).T.astype(BF16)
        bias += [bias_g] * R

    kw = kw_ref[0, pl.ds(win_start, WIN_KEYS), :]
    s_w = jnp.where(mask_w, _dot_nt(q_all, kw).reshape(H, QB, WIN_KEYS), NEG_INF)
    s_w = s_w.reshape(H * QB, WIN_KEYS)
    e_w = jnp.exp2(s_w - jnp.max(s_w, axis=-1, keepdims=True))
    acc_win = _dot(e_w.astype(BF16), with_ones(vw_ref[0, pl.ds(win_start, WIN_KEYS), :]))

    lhs_ref[:, :LANES] = q_all
    lhs_ref[:, LANES:] = jnp.concatenate(bias, axis=0)
    s_d = _dot_nt(q_all, ks_ref[0, pl.ds(start, QB), :]).reshape(H, QB, QB)
    s_d = jnp.where(tri, s_d, NEG_INF).reshape(H * QB, QB)
    m_run = jnp.max(s_d, axis=-1, keepdims=True)
    p_d = jnp.exp2(s_d - m_run).astype(BF16)
    group_rows = [slice(g * R * QB, (g + 1) * R * QB) for g in range(NSA_KV_GROUPS)]

    def group_values(v):
        return [jnp.where(low, v, 1.0), jnp.where(low, 1.0, v)]

    v_d = group_values(vs_ref[0, pl.ds(start, QB), :])
    carry = []
    for g, rows in enumerate(group_rows):
        carry += [m_run[rows], _dot(p_d[rows], v_d[g])]

    def sel_body(kb, carry):
        k0 = pl.multiple_of(kb * SEL_KEY_CHUNK, SEL_KEY_CHUNK)
        ke = jnp.concatenate([ks_ref[0, pl.ds(k0, SEL_KEY_CHUNK), :],
                              onehot_ref[pl.ds(k0, SEL_KEY_CHUNK), :]], axis=1)
        s = _dot_nt(lhs_ref[...], ke)
        v = group_values(vs_ref[0, pl.ds(k0, SEL_KEY_CHUNK), :])
        out = []
        for g, rows in enumerate(group_rows):
            m_run, acc = carry[2 * g:2 * g + 2]
            m_new = jnp.maximum(m_run, jnp.max(s[rows], axis=-1, keepdims=True))
            p = jnp.exp2(s[rows] - m_new).astype(BF16)
            out += [m_new, jnp.exp2(m_run - m_new) * acc + _dot(p, v[g])]
        return tuple(out)

    carry = lax.fori_loop(0, n_chunks, sel_body, tuple(carry))

    gates = gates_ref[0].astype(BF16)
    g_cmp, g_sel, g_win = (_dot(gates, gexp_ref[k]) for k in range(3))

    def pair(x, r):
        x = x.reshape(H, QB, x.shape[-1])
        return jnp.where(low, x[r], x[R + r])

    sel0 = carry[1].reshape(R, QB, LANES)
    sel1 = carry[3].reshape(R, QB, LANES)
    tiles = []
    for r in range(R):
        cols = slice(r * LANES, (r + 1) * LANES)
        sums = pltpu.roll(jnp.where(low, sel1[r], sel0[r]), HEAD_DIM, axis=1)
        o_sel = jnp.where(low, sel0[r], sel1[r]) * (1.0 / sums)
        o_win = pair(acc_win[:, :LANES], r) * (1.0 / pair(acc_win[:, LANES:], r))
        tiles.append(g_cmp[:, cols] * pair(o_cmp, r) + g_sel[:, cols] * o_sel + g_win[:, cols] * o_win)
    o_ref[0] = jnp.concatenate(tiles, axis=1).astype(o_ref.dtype)


def _nsa(q, kc, vc, ks, vs, kw, vw, gates, overlap, onehot, gate_expand):
    B, H, S, _ = q.shape
    ncp = kc.shape[1]
    ns = overlap.shape[1]
    nq = S // Q_BLOCK
    whole = lambda n: pl.BlockSpec((1, n, KV_PAIR), lambda b, c: (b, 0, 0))
    return pl.pallas_call(
        functools.partial(_nsa_kernel, n_sel=min(SEL_TOP_N, ns)),
        grid=(B, nq),
        in_specs=[
            pl.BlockSpec((1, H, Q_BLOCK, LANES), lambda b, c: (b, 0, c, 0)),
            whole(ncp), whole(ncp), whole(S), whole(S), whole(S), whole(S),
            pl.BlockSpec((1, Q_BLOCK, LANES), lambda b, c: (b, c, 0)),
            pl.BlockSpec(overlap.shape, lambda b, c: (0, 0)),
            pl.BlockSpec(onehot.shape, lambda b, c: (0, 0)),
            pl.BlockSpec(gate_expand.shape, lambda b, c: (0, 0, 0)),
        ],
        out_specs=pl.BlockSpec((1, Q_BLOCK, H * HEAD_DIM), lambda b, c: (b, c, 0)),
        out_shape=jax.ShapeDtypeStruct((B, S, H * HEAD_DIM), BF16),
        scratch_shapes=[pltpu.VMEM((H * Q_BLOCK, LANES + ns), BF16)],
        compiler_params=pltpu.CompilerParams(
            dimension_semantics=("parallel", "arbitrary"), vmem_limit_bytes=VMEM_LIMIT),
        name="nsa",
    )(q, kc, vc, ks, vs, kw, vw, gates, overlap, onehot, gate_expand)


def _merge_kernel(x_ref, nsa_ref, sgu_ref, gate_ref, wbn_ref, wbs_ref, wout_ref, g_ref, b_ref,
                  o_ref, *, alpha):
    d = x_ref.shape[1]
    a = _dot(nsa_ref[...], wbn_ref[...])
    s = _dot(sgu_ref[...], wbs_ref[...])
    merged = gate_ref[:, :d].astype(F32) * a + gate_ref[:, d:].astype(F32) * s
    mix = _dot(merged.astype(BF16), wout_ref[...])
    o_ref[...] = _layer_norm(alpha * x_ref[...] + mix, g_ref[...], b_ref[...])


def _merge(x2, o_nsa, o_sgu, merge_g, wbn, wbs, wout, ln_g, ln_b, alpha):
    T, D = x2.shape
    tm = TOKEN_TILE
    row = lambda w: pl.BlockSpec((tm, w), lambda i: (i, 0))
    full = lambda a: pl.BlockSpec(a.shape, lambda i: (0, 0))
    return pl.pallas_call(
        functools.partial(_merge_kernel, alpha=alpha),
        grid=(T // tm,),
        in_specs=[row(D), row(o_nsa.shape[1]), row(o_sgu.shape[1]), row(2 * D),
                  full(wbn), full(wbs), full(wout), full(ln_g), full(ln_b)],
        out_specs=row(D),
        out_shape=jax.ShapeDtypeStruct((T, D), F32),
        compiler_params=pltpu.CompilerParams(
            dimension_semantics=("parallel",), vmem_limit_bytes=VMEM_LIMIT),
        name="merge",
    )(x2, o_nsa, o_sgu, merge_g, wbn, wbs, wout, ln_g, ln_b)


def _ffn_kernel(h_ref, wg_ref, wu_ref, wd_ref, g_ref, b_ref, o_ref, hb_ref, acc_ref, *, alpha):
    j = pl.program_id(1)

    @pl.when(j == 0)
    def _():
        hb_ref[...] = h_ref[...].astype(BF16)
        acc_ref[...] = jnp.zeros_like(acc_ref)

    hb = hb_ref[...]
    act = jax.nn.silu(_dot(hb, wg_ref[...])) * _dot(hb, wu_ref[...])
    acc_ref[...] += _dot(act.astype(BF16), wd_ref[...])

    @pl.when(j == pl.num_programs(1) - 1)
    def _():
        o_ref[...] = _layer_norm(alpha * h_ref[...] + acc_ref[...], g_ref[...], b_ref[...])


def _ffn(h, wg, wu, wd, ln_g, ln_b, alpha):
    T, D = h.shape
    hidden = wg.shape[1]
    tm = TOKEN_TILE
    n_h = 2
    th = hidden // n_h
    assert th * n_h == hidden and th % LANES == 0
    return pl.pallas_call(
        functools.partial(_ffn_kernel, alpha=alpha),
        grid=(T // tm, n_h),
        in_specs=[
            pl.BlockSpec((tm, D), lambda i, j: (i, 0)),
            pl.BlockSpec((D, th), lambda i, j: (0, j)),
            pl.BlockSpec((D, th), lambda i, j: (0, j)),
            pl.BlockSpec((th, D), lambda i, j: (j, 0)),
            pl.BlockSpec((1, D), lambda i, j: (0, 0)),
            pl.BlockSpec((1, D), lambda i, j: (0, 0)),
        ],
        out_specs=pl.BlockSpec((tm, D), lambda i, j: (i, 0)),
        out_shape=jax.ShapeDtypeStruct((T, D), F32),
        scratch_shapes=[pltpu.VMEM((tm, D), BF16), pltpu.VMEM((tm, D), F32)],
        compiler_params=pltpu.CompilerParams(
            dimension_semantics=("parallel", "arbitrary"), vmem_limit_bytes=VMEM_LIMIT),
        name="ffn",
    )(h, wg, wu, wd, ln_g, ln_b)


def _split_sizes(d_model):
    return [Q_WIDTH] + [KV_PAIR] * 6 + [3 * NSA_HEADS, SGU_WIDTH, SGU_WIDTH, 2 * d_model]


def _compress_weights(w1):
    hid = w1.shape[1]
    w = w1.reshape(2, CMP_STRIDE, 1, HEAD_DIM, hid)
    eye = jnp.eye(NSA_KV_GROUPS, dtype=w1.dtype)
    wg = w[:, None] * eye[None, :, None, :, None, None]
    return wg.reshape(2, NSA_KV_GROUPS, CMP_STRIDE * KV_PAIR, hid).astype(BF16)


def _compress_pe(pe):
    p = jnp.broadcast_to(pe.reshape(2, CMP_STRIDE, 1, HEAD_DIM), (2, CMP_STRIDE, NSA_KV_GROUPS, HEAD_DIM))
    return p.reshape(2, CMP_STRIDE * KV_PAIR)


def kernel(x, positions, w_in, pe_ck, w_ck1, w_ck2, pe_cv, w_cv1, w_cv2, ln_sgu_g, ln_sgu_b,
           w_spatial, b_spatial, w_branch_nsa, w_branch_sgu, w_out, ln1_g, ln1_b,
           w_ffn_gate, w_ffn_up, w_ffn_down, ln2_g, ln2_b):
    B, S, D = x.shape
    depth = w_in.shape[0]
    alpha = (2.0 * depth) ** 0.25
    assert S % TOKEN_TILE == 0 and S % SEL_KEY_CHUNK == 0 and S >= WIN_KEYS
    ncp = S // CMP_STRIDE
    ns = S // SEL_BLOCK

    freqs = ROPE_THETA ** (-jnp.arange(ROPE_HALF, dtype=F32) / ROPE_HALF)
    freqs = freqs.reshape(ROPE_HALF, 1)
    pos_rows = positions.astype(F32).reshape(B * S // TOKEN_TILE, 1, TOKEN_TILE)

    cs = np.arange(ncp)[:, None] * CMP_STRIDE
    ss = np.arange(ns)[None, :] * SEL_BLOCK
    overlap = np.clip(np.minimum(cs + CMP_BLOCK, ss + SEL_BLOCK) - np.maximum(cs, ss), 0, None)
    overlap = jnp.asarray(overlap.astype(np.float32) / CMP_BLOCK)
    onehot = jnp.asarray((np.arange(S)[:, None] // SEL_BLOCK) == np.arange(ns)[None, :], dtype=BF16)

    gate_expand = np.zeros((3, LANES, Q_WIDTH), np.float32)
    for head in range(NSA_HEADS):
        grp, r = divmod(head, NSA_GROUP_SIZE)
        lo = r * LANES + grp * HEAD_DIM
        for k in range(3):
            gate_expand[k, 3 * head + k, lo:lo + HEAD_DIM] = 1.0
    gate_expand = jnp.asarray(gate_expand, dtype=BF16)

    sizes = _split_sizes(D)
    offs = np.concatenate([[0], np.cumsum(sizes)])
    seg = lambda w, i: w[:, offs[i]:offs[i + 1]]

    h = x.reshape(B * S, D)
    for l in range(depth):
        w = w_in[l]
        gate_cols = jnp.pad(seg(w, 7), ((0, 0), (0, LANES - sizes[7])))
        w_all = jnp.concatenate([seg(w, i) for i in (0, 1, 2, 3, 4, 5, 6, 8, 9, 10)] + [gate_cols],
                                axis=1).astype(BF16)
        bsp = jnp.repeat(b_spatial[l].T, SGU_WIDTH // SGU_GROUPS, axis=1)
        q, k_c, v_c, k_s, v_s, k_w, v_w, gates, o_sgu, merge_g = _inproj(
            h.reshape(B, S, D), pos_rows, freqs, w_all,
            ln_sgu_g[l].reshape(1, -1), ln_sgu_b[l].reshape(1, -1), w_spatial[l], bsp)

        kc, vc = _compress(
            k_c.reshape(B, ncp, CMP_STRIDE * KV_PAIR), v_c.reshape(B, ncp, CMP_STRIDE * KV_PAIR),
            _compress_pe(pe_ck[l]), _compress_weights(w_ck1[l]), w_ck2[l].astype(BF16),
            _compress_pe(pe_cv[l]), _compress_weights(w_cv1[l]), w_cv2[l].astype(BF16))

        o_nsa = _nsa(q, kc, vc, k_s, v_s, k_w, v_w, gates, overlap, onehot, gate_expand)

        wbn = w_branch_nsa[l].reshape(NSA_KV_GROUPS, NSA_GROUP_SIZE, HEAD_DIM, -1)
        wbn = wbn.transpose(1, 0, 2, 3).reshape(Q_WIDTH, -1)
        h = _merge(h, o_nsa.reshape(B * S, -1), o_sgu, merge_g,
                   wbn.astype(BF16), w_branch_sgu[l].astype(BF16), w_out[l].astype(BF16),
                   ln1_g[l].reshape(1, -1), ln1_b[l].reshape(1, -1), alpha)
        h = _ffn(h, w_ffn_gate[l].astype(BF16), w_ffn_up[l].astype(BF16), w_ffn_down[l].astype(BF16),
                 ln2_g[l].reshape(1, -1), ln2_b[l].reshape(1, -1), alpha)
    return h.reshape(B, S, D)
```

```python
import functools

import numpy as np
import jax
import jax.numpy as jnp
from jax import lax
from jax.experimental import pallas as pl
from jax.experimental.pallas import tpu as pltpu

F32 = jnp.float32
BF16 = jnp.bfloat16

HEAD_DIM = 64
NSA_HEADS = 8
NSA_KV_GROUPS = 2
NSA_GROUP_SIZE = NSA_HEADS // NSA_KV_GROUPS
CMP_BLOCK = 32
CMP_STRIDE = 16
CMP_HIDDEN = 128
SEL_BLOCK = 64
SEL_TOP_N = 16
WINDOW = 512
Q_BLOCK = 128
FORCE_BONUS = 1.0e4
N_FORCED = 3
assert FORCE_BONUS > NSA_GROUP_SIZE and SEL_TOP_N >= N_FORCED
ROPE_THETA = 500000.0
ROPE_DIM = HEAD_DIM // 4
ROPE_HALF = ROPE_DIM // 2
SGU_GROUPS = 4
SGU_CHUNK = 128
SGU_WIDTH = 512
LN_EPS = 1e-5
NEG_INF = -1e30

LANES = 128
KV_PAIR = NSA_KV_GROUPS * HEAD_DIM
assert KV_PAIR == LANES

TOKEN_TILE = 512
FFN_TOKEN_TILE = 512
FFN_HIDDEN_TILE = 1408
SEL_KEY_CHUNK = 1024
WIN_KEYS = WINDOW + Q_BLOCK
VMEM_LIMIT = 56 * 1024 * 1024


def _layer_norm(x, g, b):
    mu = jnp.mean(x, axis=-1, keepdims=True)
    xc = x - mu
    var = jnp.mean(xc * xc, axis=-1, keepdims=True)
    return xc * lax.rsqrt(var + LN_EPS) * g + b


def _dot(a, b):
    return jnp.dot(a, b, preferred_element_type=F32)


def _dot_nt(a, b):
    return lax.dot_general(a, b, (((1,), (1,)), ((), ())), preferred_element_type=F32)


Q_WIDTH = NSA_HEADS * HEAD_DIM
COL_Q = 0
COL_KV = COL_Q + Q_WIDTH
COL_U = COL_KV + 6 * KV_PAIR
COL_V = COL_U + SGU_WIDTH
COL_MERGE = COL_V + SGU_WIDTH
ROPED_KV = (0, 2, 4)
LOG2_E = 1.4426950408889634


def _inproj_kernel(x_ref, pos_ref, freq_ref, w_ref, lng_ref, lnb_ref, wsp_ref, bsp_ref,
                   q_ref, kc_ref, vc_ref, ks_ref, vs_ref, kw_ref, vw_ref,
                   gates_ref, sgu_ref, merge_ref, *, d_model):
    tm = x_ref.shape[0]
    col_gates = COL_MERGE + 2 * d_model
    xb = x_ref[...].astype(BF16)

    ang = freq_ref[...] * pos_ref[0]
    cos_f = jnp.cos(ang)
    sin_f = jnp.sin(ang)
    rest = HEAD_DIM - ROPE_DIM
    per_head = lambda parts: jnp.concatenate(parts * (LANES // HEAD_DIM), axis=0).T
    cos_t = per_head([cos_f, cos_f, jnp.ones((rest, tm), F32)])
    sin_lo = per_head([-sin_f, jnp.zeros((ROPE_HALF + rest, tm), F32)])
    sin_hi = per_head([jnp.zeros((ROPE_HALF, tm), F32), sin_f, jnp.zeros((rest, tm), F32)])
    lane = lax.broadcasted_iota(jnp.int32, (1, LANES), 1)

    def rope(t):
        return (t * cos_t + pltpu.roll(t, LANES - ROPE_HALF, axis=1) * sin_lo
                + pltpu.roll(t, ROPE_HALF, axis=1) * sin_hi)

    rq = _dot(xb, w_ref[:, COL_Q:COL_Q + Q_WIDTH])
    scale = HEAD_DIM ** -0.5 * LOG2_E
    for pair in range(NSA_HEADS // 2):
        t = rope(rq[:, pair * LANES:(pair + 1) * LANES]) * scale
        t_sw = pltpu.roll(t, HEAD_DIM, axis=1)
        for half in range(2):
            h = 2 * pair + half
            grp = h // NSA_GROUP_SIZE
            src = t if half == grp else t_sw
            keep = (lane >= HEAD_DIM) if grp == 1 else (lane < HEAD_DIM)
            q_ref[0, h] = jnp.where(keep, src, 0.0).astype(q_ref.dtype)

    rkv = _dot(xb, w_ref[:, COL_KV:COL_KV + 6 * KV_PAIR])
    kv_refs = (kc_ref, vc_ref, ks_ref, vs_ref, kw_ref, vw_ref)
    for i, ref in enumerate(kv_refs):
        t = rkv[:, i * KV_PAIR:(i + 1) * KV_PAIR]
        if i in ROPED_KV:
            t = rope(t)
        ref[0] = t.astype(ref.dtype)

    u = jax.nn.gelu(_dot(xb, w_ref[:, COL_U:COL_U + SGU_WIDTH]))
    v = jax.nn.gelu(_dot(xb, w_ref[:, COL_V:COL_V + SGU_WIDTH]))
    v = _layer_norm(v, lng_ref[...], lnb_ref[...]).astype(BF16)
    row = lax.broadcasted_iota(jnp.int32, (SGU_CHUNK, SGU_CHUNK), 0)
    col = lax.broadcasted_iota(jnp.int32, (SGU_CHUNK, SGU_CHUNK), 1)
    gdim = SGU_WIDTH // SGU_GROUPS
    w_sp = [jnp.where(col <= row, wsp_ref[g], 0.0).astype(BF16) for g in range(SGU_GROUPS)]
    for n in range(tm // SGU_CHUNK):
        rows = slice(n * SGU_CHUNK, (n + 1) * SGU_CHUNK)
        mixed = jnp.concatenate(
            [_dot(w_sp[g], v[rows, g * gdim:(g + 1) * gdim]) for g in range(SGU_GROUPS)], axis=1)
        sgu_ref[rows, :] = (u[rows, :] * (mixed + bsp_ref[...])).astype(sgu_ref.dtype)

    rm = _dot(xb, w_ref[:, COL_MERGE:COL_MERGE + 2 * d_model])
    merge_ref[...] = jax.nn.sigmoid(rm).astype(merge_ref.dtype)
    rg = _dot(xb, w_ref[:, col_gates:col_gates + LANES])
    gates_ref[0] = jax.nn.sigmoid(rg)


def _inproj(x, pos_rows, freqs, w_all, ln_g, ln_b, w_sp, b_sp):
    B, S, D = x.shape
    tm = TOKEN_TILE
    nt = S // tm
    wcols = w_all.shape[1]
    x2 = x.reshape(B * S, D)
    kv_shape = jax.ShapeDtypeStruct((B, S, KV_PAIR), BF16)
    kv_spec = pl.BlockSpec((1, tm, KV_PAIR), lambda b, i: (b, i, 0))
    const2 = lambda b, i: (0, 0)
    return pl.pallas_call(
        functools.partial(_inproj_kernel, d_model=D),
        grid=(B, nt),
        in_specs=[
            pl.BlockSpec((tm, D), lambda b, i: (b * nt + i, 0)),
            pl.BlockSpec((1, 1, tm), lambda b, i: (b * nt + i, 0, 0)),
            pl.BlockSpec((ROPE_HALF, 1), const2),
            pl.BlockSpec((D, wcols), const2),
            pl.BlockSpec((1, SGU_WIDTH), const2),
            pl.BlockSpec((1, SGU_WIDTH), const2),
            pl.BlockSpec((SGU_GROUPS, SGU_CHUNK, SGU_CHUNK), lambda b, i: (0, 0, 0)),
            pl.BlockSpec((SGU_CHUNK, SGU_WIDTH), const2),
        ],
        out_specs=[
            pl.BlockSpec((1, NSA_HEADS, tm, LANES), lambda b, i: (b, 0, i, 0)),
            kv_spec, kv_spec, kv_spec, kv_spec, kv_spec, kv_spec,
            pl.BlockSpec((1, tm, LANES), lambda b, i: (b, i, 0)),
            pl.BlockSpec((tm, SGU_WIDTH), lambda b, i: (b * nt + i, 0)),
            pl.BlockSpec((tm, 2 * D), lambda b, i: (b * nt + i, 0)),
        ],
        out_shape=[
            jax.ShapeDtypeStruct((B, NSA_HEADS, S, LANES), BF16),
            kv_shape, kv_shape, kv_shape, kv_shape, kv_shape, kv_shape,
            jax.ShapeDtypeStruct((B, S, LANES), F32),
            jax.ShapeDtypeStruct((B * S, SGU_WIDTH), BF16),
            jax.ShapeDtypeStruct((B * S, 2 * D), BF16),
        ],
        compiler_params=pltpu.CompilerParams(
            dimension_semantics=("parallel", "parallel"), vmem_limit_bytes=VMEM_LIMIT),
        name="inproj",
    )(x2, pos_rows, freqs, w_all, ln_g, ln_b, w_sp, b_sp)


def _compress_kernel(k_ref, v_ref, kpe_ref, kw1_ref, kw2_ref, vpe_ref, vw1_ref, vw2_ref,
                     ko_ref, vo_ref):
    ncp = k_ref.shape[1]

    def one(tok_ref, pe_ref, w1_ref, w2_ref, out_ref):
        a = tok_ref[0]
        width = pe_ref.shape[1]
        pe_first = jnp.broadcast_to(pe_ref[0:1, :], (8, width)).astype(BF16)
        pe_second = jnp.broadcast_to(pe_ref[1:2, :], (8, width)).astype(BF16)
        outs = []
        for g in range(NSA_KV_GROUPS):
            first = _dot(a, w1_ref[0, g])
            second = _dot(a, w1_ref[1, g])
            second = pltpu.roll(second, ncp - 1, axis=0)
            bias = (_dot(pe_first, w1_ref[0, g]) + _dot(pe_second, w1_ref[1, g]))[0:1]
            hid = jax.nn.silu(first + second + bias).astype(BF16)
            outs.append(_dot(hid, w2_ref[...]))
        out_ref[0] = jnp.concatenate(outs, axis=1).astype(out_ref.dtype)

    one(k_ref, kpe_ref, kw1_ref, kw2_ref, ko_ref)
    one(v_ref, vpe_ref, vw1_ref, vw2_ref, vo_ref)


def _compress(k_tok, v_tok, kpe, kw1, kw2, vpe, vw1, vw2):
    B, ncp, width = k_tok.shape
    tok_spec = pl.BlockSpec((1, ncp, width), lambda b: (b, 0, 0))
    pe_spec = pl.BlockSpec(kpe.shape, lambda b: (0, 0))
    w1_spec = pl.BlockSpec(kw1.shape, lambda b: (0, 0, 0, 0))
    w2_spec = pl.BlockSpec(kw2.shape, lambda b: (0, 0))
    out_spec = pl.BlockSpec((1, ncp, KV_PAIR), lambda b: (b, 0, 0))
    out_shape = jax.ShapeDtypeStruct((B, ncp, KV_PAIR), BF16)
    return pl.pallas_call(
        _compress_kernel,
        grid=(B,),
        in_specs=[tok_spec, tok_spec, pe_spec, w1_spec, w2_spec, pe_spec, w1_spec, w2_spec],
        out_specs=[out_spec, out_spec],
        out_shape=[out_shape, out_shape],
        compiler_params=pltpu.CompilerParams(
            dimension_semantics=("parallel",), vmem_limit_bytes=VMEM_LIMIT),
        name="compress",
    )(k_tok, v_tok, kpe, kw1, kw2, vpe, vw1, vw2)


def _softmax_rows(s, mask, row_valid):
    sm = jnp.where(mask, s, NEG_INF)
    m = jnp.max(sm, axis=-1, keepdims=True)
    e = jnp.exp2(sm - m)
    return e * jnp.where(row_valid, 1.0 / jnp.sum(e, axis=-1, keepdims=True), 0.0)


KNOCKED_OUT = -(2.0 ** 127)


def _top_rows(s, row_idx, rounds):
    n = float(s.shape[0])
    row_idx = row_idx.astype(F32)
    for _ in range(rounds):
        m = jnp.max(s, axis=0, keepdims=True)
        first = jnp.min(jnp.where(s == m, row_idx, n), axis=0, keepdims=True)
        s = jnp.where(row_idx == first, KNOCKED_OUT, s)
    return s <= KNOCKED_OUT


def _nsa_kernel(q_ref, kc_ref, vc_ref, ks_ref, vs_ref, kw_ref, vw_ref, gates_ref,
                ovl_ref, onehot_ref, gexp_ref, o_ref, lhs_ref, *, n_sel):
    R = NSA_GROUP_SIZE
    QB = Q_BLOCK
    ncp = kc_ref.shape[1]
    ns = ovl_ref.shape[1]
    c = pl.program_id(1)
    start = c * QB
    t_col = start + lax.broadcasted_iota(jnp.int32, (QB, 1), 0)
    lane = lax.broadcasted_iota(jnp.int32, (1, LANES), 1)

    kc = kc_ref[0]
    vc = vc_ref[0]
    cmp_end = lax.broadcasted_iota(jnp.int32, (1, ncp), 1) * CMP_STRIDE + (CMP_BLOCK - 1)
    mask_c = (cmp_end <= t_col)[None]

    row_valid = (t_col >= CMP_BLOCK - 1)[None]

    cur_row = (start + lax.broadcasted_iota(jnp.int32, (1, QB), 1)) // SEL_BLOCK
    j_row = lax.broadcasted_iota(jnp.int32, (ns, QB), 0)
    valid_t = j_row <= cur_row
    forced_t = (j_row == 0) | (j_row == cur_row) | (j_row == cur_row - 1)
    free_t = valid_t & jnp.logical_not(forced_t)

    win_start = pl.multiple_of(jnp.maximum(start - WINDOW, 0), QB)
    wpos = win_start + lax.broadcasted_iota(jnp.int32, (1, WIN_KEYS), 1)
    mask_w = ((wpos <= t_col) & (wpos > t_col - WINDOW))[None]

    first_own_block = start // SEL_BLOCK
    tri = (lax.broadcasted_iota(jnp.int32, (1, QB, QB), 2)
           <= lax.broadcasted_iota(jnp.int32, (1, QB, QB), 1))

    H = NSA_HEADS
    q_all = q_ref[0].reshape(H * QB, LANES)
    ones = jnp.ones((WIN_KEYS, LANES), BF16)
    low = lane < HEAD_DIM

    def with_ones(v):
        return jnp.concatenate([v, ones[:v.shape[0]]], axis=1)

    group_rows = [slice(g * R * QB, (g + 1) * R * QB) for g in range(NSA_KV_GROUPS)]

    def by_group(f):
        return jnp.concatenate([f(rows) for rows in group_rows], axis=0)

    s_c = by_group(lambda rows: _dot_nt(q_all[rows], kc)).reshape(H, QB, ncp)
    p_c = _softmax_rows(s_c, mask_c, row_valid)
    p_c_rows = p_c.reshape(H * QB, ncp).astype(BF16)
    o_cmp = by_group(lambda rows: _dot(p_c_rows[rows], vc)).reshape(H, QB, LANES)
    bias = []
    for g in range(NSA_KV_GROUPS):
        imp_t = _dot(jnp.sum(p_c[g * R:(g + 1) * R], axis=0), ovl_ref[...]).T
        picked_t = forced_t | (_top_rows(jnp.where(free_t, imp_t, NEG_INF), j_row,
                                         n_sel - N_FORCED) & valid_t)
        bias_g = jnp.where(picked_t & (j_row < first_own_block), 0.0, NEG_INF).T.astype(BF16)
        bias += [bias_g] * R

    kw = kw_ref[0, pl.ds(win_start, WIN_KEYS), :]
    s_w = by_group(lambda rows: _dot_nt(q_all[rows], kw)).reshape(H, QB, WIN_KEYS)
    s_w = jnp.where(mask_w, s_w, NEG_INF).reshape(H * QB, WIN_KEYS)
    e_w = jnp.exp2(s_w - jnp.max(s_w, axis=-1, keepdims=True))
    e_w = e_w.astype(BF16)
    vw1 = with_ones(vw_ref[0, pl.ds(win_start, WIN_KEYS), :])
    acc_win = by_group(lambda rows: _dot(e_w[rows], vw1))

    lhs_ref[:, :LANES] = q_all
    lhs_ref[:, LANES:] = jnp.concatenate(bias, axis=0)
    k_d = ks_ref[0, pl.ds(start, QB), :]
    s_d = by_group(lambda rows: _dot_nt(q_all[rows], k_d)).reshape(H, QB, QB)
    s_d = jnp.where(tri, s_d, NEG_INF).reshape(H * QB, QB)
    m_run = jnp.max(s_d, axis=-1, keepdims=True)
    p_d = jnp.exp2(s_d - m_run).astype(BF16)

    def group_values(v):
        return [jnp.where(low, v, 1.0), jnp.where(low, 1.0, v)]

    v_d = group_values(vs_ref[0, pl.ds(start, QB), :])
    carry = []
    for g, rows in enumerate(group_rows):
        carry += [m_run[rows], _dot(p_d[rows], v_d[g])]

    def sel_body(kb, carry):
        k0 = pl.multiple_of(kb * SEL_KEY_CHUNK, SEL_KEY_CHUNK)
        ke = jnp.concatenate([ks_ref[0, pl.ds(k0, SEL_KEY_CHUNK), :],
                              onehot_ref[pl.ds(k0, SEL_KEY_CHUNK), :]], axis=1)
        s = _dot_nt(lhs_ref[...], ke)
        v = group_values(vs_ref[0, pl.ds(k0, SEL_KEY_CHUNK), :])
        out = []
        for g, rows in enumerate(group_rows):
            m_run, acc = carry[2 * g:2 * g + 2]
            m_new = jnp.maximum(m_run, jnp.max(s[rows], axis=-1, keepdims=True))
            p = jnp.exp2(s[rows] - m_new).astype(BF16)
            out += [m_new, jnp.exp2(m_run - m_new) * acc + _dot(p, v[g])]
        return tuple(out)

    def sel_body_pair(kp, carry):
        return sel_body(2 * kp + 1, sel_body(2 * kp, carry))

    n_chunks = (start + SEL_KEY_CHUNK - 1) // SEL_KEY_CHUNK
    carry = lax.fori_loop(0, n_chunks // 2, sel_body_pair, tuple(carry))
    carry = lax.fori_loop(n_chunks - n_chunks % 2, n_chunks, sel_body, carry)

    gates = gates_ref[0].astype(BF16)
    g_cmp, g_sel, g_win = (_dot(gates, gexp_ref[k]) for k in range(3))

    def pair(x, r):
        x = x.reshape(H, QB, x.shape[-1])
        return jnp.where(low, x[r], x[R + r])

    sel0 = carry[1].reshape(R, QB, LANES)
    sel1 = carry[3].reshape(R, QB, LANES)
    tiles = []
    for r in range(R):
        cols = slice(r * LANES, (r + 1) * LANES)
        sums = pltpu.roll(jnp.where(low, sel1[r], sel0[r]), HEAD_DIM, axis=1)
        o_sel = jnp.where(low, sel0[r], sel1[r]) * (1.0 / sums)
        o_win = pair(acc_win[:, :LANES], r) * (1.0 / pair(acc_win[:, LANES:], r))
        tiles.append(g_cmp[:, cols] * pair(o_cmp, r) + g_sel[:, cols] * o_sel + g_win[:, cols] * o_win)
    o_ref[0] = jnp.concatenate(tiles, axis=1).astype(o_ref.dtype)


def _nsa(q, kc, vc, ks, vs, kw, vw, gates, overlap, onehot, gate_expand):
    B, H, S, _ = q.shape
    ncp = kc.shape[1]
    ns = overlap.shape[1]
    nq = S // Q_BLOCK
    whole = lambda n: pl.BlockSpec((1, n, KV_PAIR), lambda b, c: (b, 0, 0))
    return pl.pallas_call(
        functools.partial(_nsa_kernel, n_sel=min(SEL_TOP_N, ns)),
        grid=(B, nq),
        in_specs=[
            pl.BlockSpec((1, H, Q_BLOCK, LANES), lambda b, c: (b, 0, c, 0)),
            whole(ncp), whole(ncp), whole(S), whole(S), whole(S), whole(S),
            pl.BlockSpec((1, Q_BLOCK, LANES), lambda b, c: (b, c, 0)),
            pl.BlockSpec(overlap.shape, lambda b, c: (0, 0)),
            pl.BlockSpec(onehot.shape, lambda b, c: (0, 0)),
            pl.BlockSpec(gate_expand.shape, lambda b, c: (0, 0, 0)),
        ],
        out_specs=pl.BlockSpec((1, Q_BLOCK, H * HEAD_DIM), lambda b, c: (b, c, 0)),
        out_shape=jax.ShapeDtypeStruct((B, S, H * HEAD_DIM), BF16),
        scratch_shapes=[pltpu.VMEM((H * Q_BLOCK, LANES + ns), BF16)],
        compiler_params=pltpu.CompilerParams(
            dimension_semantics=("parallel", "arbitrary"), vmem_limit_bytes=VMEM_LIMIT),
        name="nsa",
    )(q, kc, vc, ks, vs, kw, vw, gates, overlap, onehot, gate_expand)


def _merge_kernel(x_ref, nsa_ref, sgu_ref, gate_ref, wbn_ref, wbs_ref, wout_ref, g_ref, b_ref,
                  o_ref, *, alpha):
    d = x_ref.shape[1]
    a = _dot(nsa_ref[...], wbn_ref[...])
    s = _dot(sgu_ref[...], wbs_ref[...])
    merged = gate_ref[:, :d].astype(F32) * a + gate_ref[:, d:].astype(F32) * s
    mix = _dot(merged.astype(BF16), wout_ref[...])
    o_ref[...] = _layer_norm(alpha * x_ref[...] + mix, g_ref[...], b_ref[...])


def _merge(x2, o_nsa, o_sgu, merge_g, wbn, wbs, wout, ln_g, ln_b, alpha):
    T, D = x2.shape
    tm = TOKEN_TILE
    row = lambda w: pl.BlockSpec((tm, w), lambda i: (i, 0))
    full = lambda a: pl.BlockSpec(a.shape, lambda i: (0, 0))
    return pl.pallas_call(
        functools.partial(_merge_kernel, alpha=alpha),
        grid=(T // tm,),
        in_specs=[row(D), row(o_nsa.shape[1]), row(o_sgu.shape[1]), row(2 * D),
                  full(wbn), full(wbs), full(wout), full(ln_g), full(ln_b)],
        out_specs=row(D),
        out_shape=jax.ShapeDtypeStruct((T, D), F32),
        compiler_params=pltpu.CompilerParams(
            dimension_semantics=("parallel",), vmem_limit_bytes=VMEM_LIMIT),
        name="merge",
    )(x2, o_nsa, o_sgu, merge_g, wbn, wbs, wout, ln_g, ln_b)


def _ffn_kernel(h_ref, wg_ref, wu_ref, wd_ref, g_ref, b_ref, o_ref, hb_ref, acc_ref, *, alpha):
    j = pl.program_id(1)

    @pl.when(j == 0)
    def _():
        hb_ref[...] = h_ref[...].astype(BF16)
        acc_ref[...] = jnp.zeros_like(acc_ref)

    hb = hb_ref[...]
    act = jax.nn.silu(_dot(hb, wg_ref[...])) * _dot(hb, wu_ref[...])
    acc_ref[...] += _dot(act.astype(BF16), wd_ref[...])

    @pl.when(j == pl.num_programs(1) - 1)
    def _():
        o_ref[...] = _layer_norm(alpha * h_ref[...] + acc_ref[...], g_ref[...], b_ref[...])


def _ffn(h, wg, wu, wd, ln_g, ln_b, alpha):
    T, D = h.shape
    hidden = wg.shape[1]
    tm = FFN_TOKEN_TILE
    th = FFN_HIDDEN_TILE
    n_h = hidden // th
    assert th * n_h == hidden and th % LANES == 0
    return pl.pallas_call(
        functools.partial(_ffn_kernel, alpha=alpha),
        grid=(T // tm, n_h),
        in_specs=[
            pl.BlockSpec((tm, D), lambda i, j: (i, 0)),
            pl.BlockSpec((D, th), lambda i, j: (0, j)),
            pl.BlockSpec((D, th), lambda i, j: (0, j)),
            pl.BlockSpec((th, D), lambda i, j: (j, 0)),
            pl.BlockSpec((1, D), lambda i, j: (0, 0)),
            pl.BlockSpec((1, D), lambda i, j: (0, 0)),
        ],
        out_specs=pl.BlockSpec((tm, D), lambda i, j: (i, 0)),
        out_shape=jax.ShapeDtypeStruct((T, D), F32),
        scratch_shapes=[pltpu.VMEM((tm, D), BF16), pltpu.VMEM((tm, D), F32)],
        compiler_params=pltpu.CompilerParams(
            dimension_semantics=("parallel", "arbitrary"), vmem_limit_bytes=VMEM_LIMIT),
        name="ffn",
    )(h, wg, wu, wd, ln_g, ln_b)


def _split_sizes(d_model):
    return [Q_WIDTH] + [KV_PAIR] * 6 + [3 * NSA_HEADS, SGU_WIDTH, SGU_WIDTH, 2 * d_model]


def _compress_weights(w1):
    hid = w1.shape[1]
    w = w1.reshape(2, CMP_STRIDE, 1, HEAD_DIM, hid)
    eye = jnp.eye(NSA_KV_GROUPS, dtype=w1.dtype)
    wg = w[:, None] * eye[None, :, None, :, None, None]
    return wg.reshape(2, NSA_KV_GROUPS, CMP_STRIDE * KV_PAIR, hid).astype(BF16)


def _compress_pe(pe):
    p = jnp.broadcast_to(pe.reshape(2, CMP_STRIDE, 1, HEAD_DIM), (2, CMP_STRIDE, NSA_KV_GROUPS, HEAD_DIM))
    return p.reshape(2, CMP_STRIDE * KV_PAIR)


def kernel(x, positions, w_in, pe_ck, w_ck1, w_ck2, pe_cv, w_cv1, w_cv2, ln_sgu_g, ln_sgu_b,
           w_spatial, b_spatial, w_branch_nsa, w_branch_sgu, w_out, ln1_g, ln1_b,
           w_ffn_gate, w_ffn_up, w_ffn_down, ln2_g, ln2_b):
    B, S, D = x.shape
    depth = w_in.shape[0]
    alpha = (2.0 * depth) ** 0.25
    assert S % TOKEN_TILE == 0 and S % SEL_KEY_CHUNK == 0 and S >= WIN_KEYS
    ncp = S // CMP_STRIDE
    ns = S // SEL_BLOCK

    freqs = ROPE_THETA ** (-jnp.arange(ROPE_HALF, dtype=F32) / ROPE_HALF)
    freqs = freqs.reshape(ROPE_HALF, 1)
    pos_rows = positions.astype(F32).reshape(B * S // TOKEN_TILE, 1, TOKEN_TILE)

    cs = np.arange(ncp)[:, None] * CMP_STRIDE
    ss = np.arange(ns)[None, :] * SEL_BLOCK
    overlap = np.clip(np.minimum(cs + CMP_BLOCK, ss + SEL_BLOCK) - np.maximum(cs, ss), 0, None)
    overlap = jnp.asarray(overlap.astype(np.float32) / CMP_BLOCK)
    onehot = jnp.asarray((np.arange(S)[:, None] // SEL_BLOCK) == np.arange(ns)[None, :], dtype=BF16)

    gate_expand = np.zeros((3, LANES, Q_WIDTH), np.float32)
    for head in range(NSA_HEADS):
        grp, r = divmod(head, NSA_GROUP_SIZE)
        lo = r * LANES + grp * HEAD_DIM
        for k in range(3):
            gate_expand[k, 3 * head + k, lo:lo + HEAD_DIM] = 1.0
    gate_expand = jnp.asarray(gate_expand, dtype=BF16)

    sizes = _split_sizes(D)
    offs = np.concatenate([[0], np.cumsum(sizes)])
    seg = lambda w, i: w[:, offs[i]:offs[i + 1]]

    h = x.reshape(B * S, D)
    for l in range(depth):
        w = w_in[l]
        gate_cols = jnp.pad(seg(w, 7), ((0, 0), (0, LANES - sizes[7])))
        w_all = jnp.concatenate([seg(w, i) for i in (0, 1, 2, 3, 4, 5, 6, 8, 9, 10)] + [gate_cols],
                                axis=1).astype(BF16)
        bsp = jnp.repeat(b_spatial[l].T, SGU_WIDTH // SGU_GROUPS, axis=1)
        q, k_c, v_c, k_s, v_s, k_w, v_w, gates, o_sgu, merge_g = _inproj(
            h.reshape(B, S, D), pos_rows, freqs, w_all,
            ln_sgu_g[l].reshape(1, -1), ln_sgu_b[l].reshape(1, -1), w_spatial[l], bsp)

        kc, vc = _compress(
            k_c.reshape(B, ncp, CMP_STRIDE * KV_PAIR), v_c.reshape(B, ncp, CMP_STRIDE * KV_PAIR),
            _compress_pe(pe_ck[l]), _compress_weights(w_ck1[l]), w_ck2[l].astype(BF16),
            _compress_pe(pe_cv[l]), _compress_weights(w_cv1[l]), w_cv2[l].astype(BF16))

        o_nsa = _nsa(q, kc, vc, k_s, v_s, k_w, v_w, gates, overlap, onehot, gate_expand)

        wbn = w_branch_nsa[l].reshape(NSA_KV_GROUPS, NSA_GROUP_SIZE, HEAD_DIM, -1)
        wbn = wbn.transpose(1, 0, 2, 3).reshape(Q_WIDTH, -1)
        h = _merge(h, o_nsa.reshape(B * S, -1), o_sgu, merge_g,
                   wbn.astype(BF16), w_branch_sgu[l].astype(BF16), w_out[l].astype(BF16),
                   ln1_g[l].reshape(1, -1), ln1_b[l].reshape(1, -1), alpha)
        h = _ffn(h, w_ffn_gate[l].astype(BF16), w_ffn_up[l].astype(BF16), w_ffn_down[l].astype(BF16),
                 ln2_g[l].reshape(1, -1), ln2_b[l].reshape(1, -1), alpha)
    return h.reshape(B, S, D)
```

```python
import functools

import numpy as np
import jax
import jax.numpy as jnp
from jax import lax
from jax.experimental import pallas as pl
from jax.experimental.pallas import tpu as pltpu

F32 = jnp.float32
BF16 = jnp.bfloat16

HEAD_DIM = 64
NSA_HEADS = 8
NSA_KV_GROUPS = 2
NSA_GROUP_SIZE = NSA_HEADS // NSA_KV_GROUPS
CMP_BLOCK = 32
CMP_STRIDE = 16
CMP_HIDDEN = 128
SEL_BLOCK = 64
SEL_TOP_N = 16
WINDOW = 512
Q_BLOCK = 128
FORCE_BONUS = 1.0e4
N_FORCED = 3
assert FORCE_BONUS > NSA_GROUP_SIZE and SEL_TOP_N >= N_FORCED
ROPE_THETA = 500000.0
ROPE_DIM = HEAD_DIM // 4
ROPE_HALF = ROPE_DIM // 2
SGU_GROUPS = 4
SGU_CHUNK = 128
SGU_WIDTH = 512
LN_EPS = 1e-5
NEG_INF = -1e30

LANES = 128
KV_PAIR = NSA_KV_GROUPS * HEAD_DIM
assert KV_PAIR == LANES

TOKEN_TILE = 512
MERGE_TOKEN_TILE = 1024
FFN_TOKEN_TILE = 512
FFN_HIDDEN_TILE = 1408
SEL_KEY_CHUNK = 1024
WIN_KEYS = WINDOW + Q_BLOCK
VMEM_LIMIT = 56 * 1024 * 1024


def _layer_norm(x, g, b):
    mu = jnp.mean(x, axis=-1, keepdims=True)
    xc = x - mu
    var = jnp.mean(xc * xc, axis=-1, keepdims=True)
    return xc * lax.rsqrt(var + LN_EPS) * g + b


def _dot(a, b):
    return jnp.dot(a, b, preferred_element_type=F32)


def _dot_nt(a, b):
    return lax.dot_general(a, b, (((1,), (1,)), ((), ())), preferred_element_type=F32)


Q_WIDTH = NSA_HEADS * HEAD_DIM
COL_Q = 0
COL_KV = COL_Q + Q_WIDTH
COL_U = COL_KV + 6 * KV_PAIR
COL_V = COL_U + SGU_WIDTH
COL_MERGE = COL_V + SGU_WIDTH
ROPED_KV = (0, 2, 4)
LOG2_E = 1.4426950408889634


def _inproj_kernel(x_ref, pos_ref, freq_ref, w_ref, lng_ref, lnb_ref, wsp_ref, bsp_ref,
                   q_ref, kc_ref, vc_ref, ks_ref, vs_ref, kw_ref, vw_ref,
                   gates_ref, sgu_ref, merge_ref, *, d_model):
    tm = x_ref.shape[0]
    col_gates = COL_MERGE + 2 * d_model
    xb = x_ref[...].astype(BF16)

    ang = freq_ref[...] * pos_ref[0]
    cos_f = jnp.cos(ang)
    sin_f = jnp.sin(ang)
    rest = HEAD_DIM - ROPE_DIM
    per_head = lambda parts: jnp.concatenate(parts * (LANES // HEAD_DIM), axis=0).T
    cos_t = per_head([cos_f, cos_f, jnp.ones((rest, tm), F32)])
    sin_lo = per_head([-sin_f, jnp.zeros((ROPE_HALF + rest, tm), F32)])
    sin_hi = per_head([jnp.zeros((ROPE_HALF, tm), F32), sin_f, jnp.zeros((rest, tm), F32)])
    lane = lax.broadcasted_iota(jnp.int32, (1, LANES), 1)

    def rope(t):
        return (t * cos_t + pltpu.roll(t, LANES - ROPE_HALF, axis=1) * sin_lo
                + pltpu.roll(t, ROPE_HALF, axis=1) * sin_hi)

    rq = _dot(xb, w_ref[:, COL_Q:COL_Q + Q_WIDTH])
    scale = HEAD_DIM ** -0.5 * LOG2_E
    for pair in range(NSA_HEADS // 2):
        t = rope(rq[:, pair * LANES:(pair + 1) * LANES]) * scale
        t_sw = pltpu.roll(t, HEAD_DIM, axis=1)
        for half in range(2):
            h = 2 * pair + half
            grp = h // NSA_GROUP_SIZE
            src = t if half == grp else t_sw
            keep = (lane >= HEAD_DIM) if grp == 1 else (lane < HEAD_DIM)
            q_ref[0, h] = jnp.where(keep, src, 0.0).astype(q_ref.dtype)

    rkv = _dot(xb, w_ref[:, COL_KV:COL_KV + 6 * KV_PAIR])
    kv_refs = (kc_ref, vc_ref, ks_ref, vs_ref, kw_ref, vw_ref)
    for i, ref in enumerate(kv_refs):
        t = rkv[:, i * KV_PAIR:(i + 1) * KV_PAIR]
        if i in ROPED_KV:
            t = rope(t)
        ref[0] = t.astype(ref.dtype)

    u = jax.nn.gelu(_dot(xb, w_ref[:, COL_U:COL_U + SGU_WIDTH]))
    v = jax.nn.gelu(_dot(xb, w_ref[:, COL_V:COL_V + SGU_WIDTH]))
    v = _layer_norm(v, lng_ref[...], lnb_ref[...]).astype(BF16)
    row = lax.broadcasted_iota(jnp.int32, (SGU_CHUNK, SGU_CHUNK), 0)
    col = lax.broadcasted_iota(jnp.int32, (SGU_CHUNK, SGU_CHUNK), 1)
    gdim = SGU_WIDTH // SGU_GROUPS
    w_sp = [jnp.where(col <= row, wsp_ref[g], 0.0).astype(BF16) for g in range(SGU_GROUPS)]
    for n in range(tm // SGU_CHUNK):
        rows = slice(n * SGU_CHUNK, (n + 1) * SGU_CHUNK)
        mixed = jnp.concatenate(
            [_dot(w_sp[g], v[rows, g * gdim:(g + 1) * gdim]) for g in range(SGU_GROUPS)], axis=1)
        sgu_ref[rows, :] = (u[rows, :] * (mixed + bsp_ref[...])).astype(sgu_ref.dtype)

    rm = _dot(xb, w_ref[:, COL_MERGE:COL_MERGE + 2 * d_model])
    merge_ref[...] = jax.nn.sigmoid(rm).astype(merge_ref.dtype)
    rg = _dot(xb, w_ref[:, col_gates:col_gates + LANES])
    gates_ref[0] = jax.nn.sigmoid(rg)


def _inproj(x, pos_rows, freqs, w_all, ln_g, ln_b, w_sp, b_sp):
    B, S, D = x.shape
    tm = TOKEN_TILE
    nt = S // tm
    wcols = w_all.shape[1]
    x2 = x.reshape(B * S, D)
    kv_shape = jax.ShapeDtypeStruct((B, S, KV_PAIR), BF16)
    kv_spec = pl.BlockSpec((1, tm, KV_PAIR), lambda b, i: (b, i, 0))
    const2 = lambda b, i: (0, 0)
    return pl.pallas_call(
        functools.partial(_inproj_kernel, d_model=D),
        grid=(B, nt),
        in_specs=[
            pl.BlockSpec((tm, D), lambda b, i: (b * nt + i, 0)),
            pl.BlockSpec((1, 1, tm), lambda b, i: (b * nt + i, 0, 0)),
            pl.BlockSpec((ROPE_HALF, 1), const2),
            pl.BlockSpec((D, wcols), const2),
            pl.BlockSpec((1, SGU_WIDTH), const2),
            pl.BlockSpec((1, SGU_WIDTH), const2),
            pl.BlockSpec((SGU_GROUPS, SGU_CHUNK, SGU_CHUNK), lambda b, i: (0, 0, 0)),
            pl.BlockSpec((SGU_CHUNK, SGU_WIDTH), const2),
        ],
        out_specs=[
            pl.BlockSpec((1, NSA_HEADS, tm, LANES), lambda b, i: (b, 0, i, 0)),
            kv_spec, kv_spec, kv_spec, kv_spec, kv_spec, kv_spec,
            pl.BlockSpec((1, tm, LANES), lambda b, i: (b, i, 0)),
            pl.BlockSpec((tm, SGU_WIDTH), lambda b, i: (b * nt + i, 0)),
            pl.BlockSpec((tm, 2 * D), lambda b, i: (b * nt + i, 0)),
        ],
        out_shape=[
            jax.ShapeDtypeStruct((B, NSA_HEADS, S, LANES), BF16),
            kv_shape, kv_shape, kv_shape, kv_shape, kv_shape, kv_shape,
            jax.ShapeDtypeStruct((B, S, LANES), F32),
            jax.ShapeDtypeStruct((B * S, SGU_WIDTH), BF16),
            jax.ShapeDtypeStruct((B * S, 2 * D), BF16),
        ],
        compiler_params=pltpu.CompilerParams(
            dimension_semantics=("parallel", "parallel"), vmem_limit_bytes=VMEM_LIMIT),
        name="inproj",
    )(x2, pos_rows, freqs, w_all, ln_g, ln_b, w_sp, b_sp)


def _compress_kernel(k_ref, v_ref, kpe_ref, kw1_ref, kw2_ref, vpe_ref, vw1_ref, vw2_ref,
                     ko_ref, vo_ref):
    ncp = k_ref.shape[1]

    def one(tok_ref, pe_ref, w1_ref, w2_ref, out_ref):
        a = tok_ref[0]
        width = pe_ref.shape[1]
        pe_first = jnp.broadcast_to(pe_ref[0:1, :], (8, width)).astype(BF16)
        pe_second = jnp.broadcast_to(pe_ref[1:2, :], (8, width)).astype(BF16)
        outs = []
        for g in range(NSA_KV_GROUPS):
            first = _dot(a, w1_ref[0, g])
            second = _dot(a, w1_ref[1, g])
            second = pltpu.roll(second, ncp - 1, axis=0)
            bias = (_dot(pe_first, w1_ref[0, g]) + _dot(pe_second, w1_ref[1, g]))[0:1]
            hid = jax.nn.silu(first + second + bias).astype(BF16)
            outs.append(_dot(hid, w2_ref[...]))
        out_ref[0] = jnp.concatenate(outs, axis=1).astype(out_ref.dtype)

    one(k_ref, kpe_ref, kw1_ref, kw2_ref, ko_ref)
    one(v_ref, vpe_ref, vw1_ref, vw2_ref, vo_ref)


def _compress(k_tok, v_tok, kpe, kw1, kw2, vpe, vw1, vw2):
    B, ncp, width = k_tok.shape
    tok_spec = pl.BlockSpec((1, ncp, width), lambda b: (b, 0, 0))
    pe_spec = pl.BlockSpec(kpe.shape, lambda b: (0, 0))
    w1_spec = pl.BlockSpec(kw1.shape, lambda b: (0, 0, 0, 0))
    w2_spec = pl.BlockSpec(kw2.shape, lambda b: (0, 0))
    out_spec = pl.BlockSpec((1, ncp, KV_PAIR), lambda b: (b, 0, 0))
    out_shape = jax.ShapeDtypeStruct((B, ncp, KV_PAIR), BF16)
    return pl.pallas_call(
        _compress_kernel,
        grid=(B,),
        in_specs=[tok_spec, tok_spec, pe_spec, w1_spec, w2_spec, pe_spec, w1_spec, w2_spec],
        out_specs=[out_spec, out_spec],
        out_shape=[out_shape, out_shape],
        compiler_params=pltpu.CompilerParams(
            dimension_semantics=("parallel",), vmem_limit_bytes=VMEM_LIMIT),
        name="compress",
    )(k_tok, v_tok, kpe, kw1, kw2, vpe, vw1, vw2)


KNOCKED_OUT = -(2.0 ** 127)


def _top_rows(s, row_idx, rounds):
    n = float(s.shape[0])
    row_idx = row_idx.astype(F32)
    for _ in range(rounds):
        m = jnp.max(s, axis=0, keepdims=True)
        first = jnp.min(jnp.where(s == m, row_idx, n), axis=0, keepdims=True)
        s = jnp.where(row_idx == first, KNOCKED_OUT, s)
    return s <= KNOCKED_OUT


def _nsa_kernel(q_ref, kc_ref, vc_ref, ks_ref, vs_ref, kw_ref, vw_ref, gates_ref,
                ovl_ref, onehot_ref, gexp_ref, tok_ref, cbias_ref, wbias_ref, tbias_ref,
                o_ref, lhs_ref, *, n_sel):
    R = NSA_GROUP_SIZE
    QB = Q_BLOCK
    ncp = kc_ref.shape[1]
    ns = ovl_ref.shape[1]
    c = pl.program_id(1)
    start = c * QB
    t_col = start + lax.broadcasted_iota(jnp.int32, (QB, 1), 0)
    lane = lax.broadcasted_iota(jnp.int32, (1, LANES), 1)

    row_valid = jnp.concatenate([t_col >= CMP_BLOCK - 1] * NSA_HEADS, axis=0)

    cur_row = (start + lax.broadcasted_iota(jnp.int32, (1, QB), 1)) // SEL_BLOCK
    j_row = lax.broadcasted_iota(jnp.int32, (ns, QB), 0)
    valid_t = j_row <= cur_row
    forced_t = (j_row == 0) | (j_row == cur_row) | (j_row == cur_row - 1)
    free_t = valid_t & jnp.logical_not(forced_t)

    win_start = pl.multiple_of(jnp.maximum(start - WINDOW, 0), QB)
    first_own_block = start // SEL_BLOCK

    H = NSA_HEADS
    q_all = q_ref[0].reshape(H * QB, LANES)
    ones = jnp.ones((WIN_KEYS, LANES), BF16)
    low = lane < HEAD_DIM

    def with_ones(v):
        return jnp.concatenate([v, ones[:v.shape[0]]], axis=1)

    group_rows = [slice(g * R * QB, (g + 1) * R * QB) for g in range(NSA_KV_GROUPS)]

    def by_group(f):
        return jnp.concatenate([f(rows) for rows in group_rows], axis=0)

    q_tok = jnp.concatenate([q_all, tok_ref[...]], axis=1)

    def masked_scores(keys, key_bias):
        k_aug = jnp.concatenate([keys, key_bias], axis=1)
        return by_group(lambda rows: _dot_nt(q_tok[rows], k_aug))

    cmp_rows = pl.multiple_of(ncp - c * (QB // CMP_STRIDE), 8)
    s_c = masked_scores(kc_ref[0], cbias_ref[pl.ds(cmp_rows, ncp), :].astype(BF16))
    e_c = jnp.exp2(s_c - jnp.max(s_c, axis=-1, keepdims=True))
    inv_c = jnp.where(row_valid, 1.0 / jnp.sum(e_c, axis=-1, keepdims=True), 0.0)
    e_c = e_c.astype(BF16)
    v_ovl = jnp.concatenate([vc_ref[0], ovl_ref[...]], axis=1)
    pv_c = (by_group(lambda rows: _dot(e_c[rows], v_ovl)) * inv_c).reshape(H, QB, LANES + ns)
    o_cmp = pv_c[:, :, :LANES]
    bias = []
    for g in range(NSA_KV_GROUPS):
        imp_t = jnp.sum(pv_c[g * R:(g + 1) * R, :, LANES:], axis=0).T
        picked_t = forced_t | (_top_rows(jnp.where(free_t, imp_t, NEG_INF), j_row,
                                         n_sel - N_FORCED) & valid_t)
        bias_g = jnp.where(picked_t & (j_row < first_own_block), 0.0, NEG_INF).T.astype(BF16)
        bias += [bias_g] * R

    s_w = masked_scores(kw_ref[0, pl.ds(win_start, WIN_KEYS), :],
                        wbias_ref[jnp.minimum(c, WINDOW // QB)])
    e_w = jnp.exp2(s_w - jnp.max(s_w, axis=-1, keepdims=True))
    e_w = e_w.astype(BF16)
    vw1 = with_ones(vw_ref[0, pl.ds(win_start, WIN_KEYS), :])
    acc_win = by_group(lambda rows: _dot(e_w[rows], vw1))

    lhs_ref[:, :LANES] = q_all
    lhs_ref[:, LANES:] = jnp.concatenate(bias, axis=0)
    s_d = masked_scores(ks_ref[0, pl.ds(start, QB), :], tbias_ref[...])
    m_run = jnp.max(s_d, axis=-1, keepdims=True)
    p_d = jnp.exp2(s_d - m_run).astype(BF16)

    def group_values(v):
        return [jnp.where(low, v, 1.0), jnp.where(low, 1.0, v)]

    v_d = group_values(vs_ref[0, pl.ds(start, QB), :])
    carry = []
    for g, rows in enumerate(group_rows):
        carry += [m_run[rows], _dot(p_d[rows], v_d[g])]

    def sel_body(kb, carry):
        k0 = pl.multiple_of(kb * SEL_KEY_CHUNK, SEL_KEY_CHUNK)
        ke = jnp.concatenate([ks_ref[0, pl.ds(k0, SEL_KEY_CHUNK), :],
                              onehot_ref[pl.ds(k0, SEL_KEY_CHUNK), :]], axis=1)
        s = _dot_nt(lhs_ref[...], ke)
        v = group_values(vs_ref[0, pl.ds(k0, SEL_KEY_CHUNK), :])
        out = []
        for g, rows in enumerate(group_rows):
            m_run, acc = carry[2 * g:2 * g + 2]
            m_new = jnp.maximum(m_run, jnp.max(s[rows], axis=-1, keepdims=True))
            p = jnp.exp2(s[rows] - m_new).astype(BF16)
            out += [m_new, jnp.exp2(m_run - m_new) * acc + _dot(p, v[g])]
        return tuple(out)

    def sel_body_pair(kp, carry):
        return sel_body(2 * kp + 1, sel_body(2 * kp, carry))

    n_chunks = (start + SEL_KEY_CHUNK - 1) // SEL_KEY_CHUNK
    carry = lax.fori_loop(0, n_chunks // 2, sel_body_pair, tuple(carry))
    carry = lax.fori_loop(n_chunks - n_chunks % 2, n_chunks, sel_body, carry)

    gates = gates_ref[0].astype(BF16)
    g_cmp, g_sel, g_win = (_dot(gates, gexp_ref[k]) for k in range(3))

    def pair(x, r):
        x = x.reshape(H, QB, x.shape[-1])
        return jnp.where(low, x[r], x[R + r])

    sel0 = carry[1].reshape(R, QB, LANES)
    sel1 = carry[3].reshape(R, QB, LANES)
    tiles = []
    for r in range(R):
        cols = slice(r * LANES, (r + 1) * LANES)
        sums = pltpu.roll(jnp.where(low, sel1[r], sel0[r]), HEAD_DIM, axis=1)
        o_sel = jnp.where(low, sel0[r], sel1[r]) * (1.0 / sums)
        o_win = pair(acc_win[:, :LANES], r) * (1.0 / pair(acc_win[:, LANES:], r))
        tiles.append(g_cmp[:, cols] * pair(o_cmp, r) + g_sel[:, cols] * o_sel + g_win[:, cols] * o_win)
    o_ref[0] = jnp.concatenate(tiles, axis=1).astype(o_ref.dtype)


def _nsa(q, kc, vc, ks, vs, kw, vw, gates, consts):
    B, H, S, _ = q.shape
    ncp = kc.shape[1]
    ns = consts[0].shape[1]
    nq = S // Q_BLOCK
    whole = lambda n: pl.BlockSpec((1, n, KV_PAIR), lambda b, c: (b, 0, 0))
    return pl.pallas_call(
        functools.partial(_nsa_kernel, n_sel=min(SEL_TOP_N, ns)),
        grid=(B, nq),
        in_specs=[
            pl.BlockSpec((1, H, Q_BLOCK, LANES), lambda b, c: (b, 0, c, 0)),
            whole(ncp), whole(ncp), whole(S), whole(S), whole(S), whole(S),
            pl.BlockSpec((1, Q_BLOCK, LANES), lambda b, c: (b, c, 0)),
        ] + [pl.BlockSpec(a.shape, lambda b, c, nd=a.ndim: (0,) * nd) for a in consts],
        out_specs=pl.BlockSpec((1, Q_BLOCK, H * HEAD_DIM), lambda b, c: (b, c, 0)),
        out_shape=jax.ShapeDtypeStruct((B, S, H * HEAD_DIM), BF16),
        scratch_shapes=[pltpu.VMEM((H * Q_BLOCK, LANES + ns), BF16)],
        compiler_params=pltpu.CompilerParams(
            dimension_semantics=("parallel", "arbitrary"), vmem_limit_bytes=VMEM_LIMIT),
        name="nsa",
    )(q, kc, vc, ks, vs, kw, vw, gates, *consts)


def _merge_kernel(x_ref, nsa_ref, sgu_ref, gate_ref, wbn_ref, wbs_ref, wout_ref, g_ref, b_ref,
                  o_ref, *, alpha):
    d = x_ref.shape[1]
    a = _dot(nsa_ref[...], wbn_ref[...])
    s = _dot(sgu_ref[...], wbs_ref[...])
    merged = gate_ref[:, :d].astype(F32) * a + gate_ref[:, d:].astype(F32) * s
    mix = _dot(merged.astype(BF16), wout_ref[...])
    o_ref[...] = _layer_norm(alpha * x_ref[...] + mix, g_ref[...], b_ref[...])


def _merge(x2, o_nsa, o_sgu, merge_g, wbn, wbs, wout, ln_g, ln_b, alpha):
    T, D = x2.shape
    tm = MERGE_TOKEN_TILE
    row = lambda w: pl.BlockSpec((tm, w), lambda i: (i, 0))
    full = lambda a: pl.BlockSpec(a.shape, lambda i: (0, 0))
    return pl.pallas_call(
        functools.partial(_merge_kernel, alpha=alpha),
        grid=(T // tm,),
        in_specs=[row(D), row(o_nsa.shape[1]), row(o_sgu.shape[1]), row(2 * D),
                  full(wbn), full(wbs), full(wout), full(ln_g), full(ln_b)],
        out_specs=row(D),
        out_shape=jax.ShapeDtypeStruct((T, D), F32),
        compiler_params=pltpu.CompilerParams(
            dimension_semantics=("parallel",), vmem_limit_bytes=VMEM_LIMIT),
        name="merge",
    )(x2, o_nsa, o_sgu, merge_g, wbn, wbs, wout, ln_g, ln_b)


def _ffn_kernel(h_ref, wg_ref, wu_ref, wd_ref, g_ref, b_ref, o_ref, hb_ref, acc_ref, *, alpha):
    j = pl.program_id(1)

    @pl.when(j == 0)
    def _():
        hb_ref[...] = h_ref[...].astype(BF16)
        acc_ref[...] = jnp.zeros_like(acc_ref)

    hb = hb_ref[...]
    act = jax.nn.silu(_dot(hb, wg_ref[...])) * _dot(hb, wu_ref[...])
    acc_ref[...] += _dot(act.astype(BF16), wd_ref[...])

    @pl.when(j == pl.num_programs(1) - 1)
    def _():
        o_ref[...] = _layer_norm(alpha * h_ref[...] + acc_ref[...], g_ref[...], b_ref[...])


def _ffn(h, wg, wu, wd, ln_g, ln_b, alpha):
    T, D = h.shape
    hidden = wg.shape[1]
    tm = FFN_TOKEN_TILE
    th = FFN_HIDDEN_TILE
    n_h = hidden // th
    assert th * n_h == hidden and th % LANES == 0
    return pl.pallas_call(
        functools.partial(_ffn_kernel, alpha=alpha),
        grid=(T // tm, n_h),
        in_specs=[
            pl.BlockSpec((tm, D), lambda i, j: (i, 0)),
            pl.BlockSpec((D, th), lambda i, j: (0, j)),
            pl.BlockSpec((D, th), lambda i, j: (0, j)),
            pl.BlockSpec((th, D), lambda i, j: (j, 0)),
            pl.BlockSpec((1, D), lambda i, j: (0, 0)),
            pl.BlockSpec((1, D), lambda i, j: (0, 0)),
        ],
        out_specs=pl.BlockSpec((tm, D), lambda i, j: (i, 0)),
        out_shape=jax.ShapeDtypeStruct((T, D), F32),
        scratch_shapes=[pltpu.VMEM((tm, D), BF16), pltpu.VMEM((tm, D), F32)],
        compiler_params=pltpu.CompilerParams(
            dimension_semantics=("parallel", "arbitrary"), vmem_limit_bytes=VMEM_LIMIT),
        name="ffn",
    )(h, wg, wu, wd, ln_g, ln_b)


def _split_sizes(d_model):
    return [Q_WIDTH] + [KV_PAIR] * 6 + [3 * NSA_HEADS, SGU_WIDTH, SGU_WIDTH, 2 * d_model]


def _nsa_constants(S):
    ncp = S // CMP_STRIDE
    ns = S // SEL_BLOCK
    QB = Q_BLOCK
    cs = np.arange(ncp)[:, None] * CMP_STRIDE
    ss = np.arange(ns)[None, :] * SEL_BLOCK
    overlap = np.clip(np.minimum(cs + CMP_BLOCK, ss + SEL_BLOCK) - np.maximum(cs, ss), 0, None) / CMP_BLOCK
    onehot = (np.arange(S)[:, None] // SEL_BLOCK) == np.arange(ns)[None, :]
    gate_expand = np.zeros((3, LANES, Q_WIDTH), np.float32)
    for head in range(NSA_HEADS):
        grp, r = divmod(head, NSA_GROUP_SIZE)
        lo = r * LANES + grp * HEAD_DIM
        for k in range(3):
            gate_expand[k, 3 * head + k, lo:lo + HEAD_DIM] = 1.0
    tok = np.tile(np.eye(QB, dtype=np.float32), (NSA_HEADS, 1))
    tl = np.arange(QB)[None, :]
    bias = lambda visible: np.where(visible, 0.0, NEG_INF).astype(np.float32)
    i_rel = np.arange(2 * ncp)[:, None] - ncp
    cmp_bias = bias(CMP_STRIDE * i_rel + CMP_BLOCK - 1 <= tl)
    kk = np.arange(WIN_KEYS)[:, None]
    n_clip = WINDOW // QB
    win_bias = np.stack([bias(kk <= c * QB + tl) for c in range(n_clip)]
                        + [bias((kk <= WINDOW + tl) & (kk > tl))])
    own_bias = bias(np.arange(QB)[:, None] <= tl)
    as_bf16 = lambda a: jnp.asarray(a, dtype=BF16)
    return (as_bf16(overlap), as_bf16(onehot), as_bf16(gate_expand), as_bf16(tok),
            jnp.asarray(cmp_bias), as_bf16(win_bias), as_bf16(own_bias))


def _compress_weights(w1):
    hid = w1.shape[1]
    w = w1.reshape(2, CMP_STRIDE, 1, HEAD_DIM, hid)
    eye = jnp.eye(NSA_KV_GROUPS, dtype=w1.dtype)
    wg = w[:, None] * eye[None, :, None, :, None, None]
    return wg.reshape(2, NSA_KV_GROUPS, CMP_STRIDE * KV_PAIR, hid).astype(BF16)


def _compress_pe(pe):
    p = jnp.broadcast_to(pe.reshape(2, CMP_STRIDE, 1, HEAD_DIM), (2, CMP_STRIDE, NSA_KV_GROUPS, HEAD_DIM))
    return p.reshape(2, CMP_STRIDE * KV_PAIR)


def kernel(x, positions, w_in, pe_ck, w_ck1, w_ck2, pe_cv, w_cv1, w_cv2, ln_sgu_g, ln_sgu_b,
           w_spatial, b_spatial, w_branch_nsa, w_branch_sgu, w_out, ln1_g, ln1_b,
           w_ffn_gate, w_ffn_up, w_ffn_down, ln2_g, ln2_b):
    B, S, D = x.shape
    depth = w_in.shape[0]
    alpha = (2.0 * depth) ** 0.25
    assert S % TOKEN_TILE == 0 and S % SEL_KEY_CHUNK == 0 and S >= WIN_KEYS
    ncp = S // CMP_STRIDE
    ns = S // SEL_BLOCK

    freqs = ROPE_THETA ** (-jnp.arange(ROPE_HALF, dtype=F32) / ROPE_HALF)
    freqs = freqs.reshape(ROPE_HALF, 1)
    pos_rows = positions.astype(F32).reshape(B * S // TOKEN_TILE, 1, TOKEN_TILE)

    nsa_consts = _nsa_constants(S)

    sizes = _split_sizes(D)
    offs = np.concatenate([[0], np.cumsum(sizes)])
    seg = lambda w, i: w[:, offs[i]:offs[i + 1]]

    h = x.reshape(B * S, D)
    for l in range(depth):
        w = w_in[l]
        gate_cols = jnp.pad(seg(w, 7), ((0, 0), (0, LANES - sizes[7])))
        w_all = jnp.concatenate([seg(w, i) for i in (0, 1, 2, 3, 4, 5, 6, 8, 9, 10)] + [gate_cols],
                                axis=1).astype(BF16)
        bsp = jnp.repeat(b_spatial[l].T, SGU_WIDTH // SGU_GROUPS, axis=1)
        q, k_c, v_c, k_s, v_s, k_w, v_w, gates, o_sgu, merge_g = _inproj(
            h.reshape(B, S, D), pos_rows, freqs, w_all,
            ln_sgu_g[l].reshape(1, -1), ln_sgu_b[l].reshape(1, -1), w_spatial[l], bsp)

        kc, vc = _compress(
            k_c.reshape(B, ncp, CMP_STRIDE * KV_PAIR), v_c.reshape(B, ncp, CMP_STRIDE * KV_PAIR),
            _compress_pe(pe_ck[l]), _compress_weights(w_ck1[l]), w_ck2[l].astype(BF16),
            _compress_pe(pe_cv[l]), _compress_weights(w_cv1[l]), w_cv2[l].astype(BF16))

        o_nsa = _nsa(q, kc, vc, k_s, v_s, k_w, v_w, gates, nsa_consts)

        wbn = w_branch_nsa[l].reshape(NSA_KV_GROUPS, NSA_GROUP_SIZE, HEAD_DIM, -1)
        wbn = wbn.transpose(1, 0, 2, 3).reshape(Q_WIDTH, -1)
        h = _merge(h, o_nsa.reshape(B * S, -1), o_sgu, merge_g,
                   wbn.astype(BF16), w_branch_sgu[l].astype(BF16), w_out[l].astype(BF16),
                   ln1_g[l].reshape(1, -1), ln1_b[l].reshape(1, -1), alpha)
        h = _ffn(h, w_ffn_gate[l].astype(BF16), w_ffn_up[l].astype(BF16), w_ffn_down[l].astype(BF16),
                 ln2_g[l].reshape(1, -1), ln2_b[l].reshape(1, -1), alpha)
    return h.reshape(B, S, D)
```

```python
import functools

import numpy as np
import jax
import jax.numpy as jnp
from jax import lax
from jax.experimental import pallas as pl
from jax.experimental.pallas import tpu as pltpu

F32 = jnp.float32
BF16 = jnp.bfloat16

HEAD_DIM = 64
NSA_HEADS = 8
NSA_KV_GROUPS = 2
NSA_GROUP_SIZE = NSA_HEADS // NSA_KV_GROUPS
CMP_BLOCK = 32
CMP_STRIDE = 16
CMP_HIDDEN = 128
SEL_BLOCK = 64
SEL_TOP_N = 16
WINDOW = 512
Q_BLOCK = 128
FORCE_BONUS = 1.0e4
N_FORCED = 3
assert FORCE_BONUS > NSA_GROUP_SIZE and SEL_TOP_N >= N_FORCED
ROPE_THETA = 500000.0
ROPE_DIM = HEAD_DIM // 4
ROPE_HALF = ROPE_DIM // 2
SGU_GROUPS = 4
SGU_CHUNK = 128
SGU_WIDTH = 512
LN_EPS = 1e-5
NEG_INF = -1e30

LANES = 128
KV_PAIR = NSA_KV_GROUPS * HEAD_DIM
assert KV_PAIR == LANES

TOKEN_TILE = 512
MERGE_TOKEN_TILE = 1024
FFN_TOKEN_TILE = 512
FFN_HIDDEN_TILE = 1408
SEL_KEY_CHUNK = 1024
WIN_KEYS = WINDOW + Q_BLOCK
VMEM_LIMIT = 56 * 1024 * 1024


def _layer_norm(x, g, b):
    mu = jnp.mean(x, axis=-1, keepdims=True)
    xc = x - mu
    var = jnp.mean(xc * xc, axis=-1, keepdims=True)
    return xc * lax.rsqrt(var + LN_EPS) * g + b


def _dot(a, b):
    return jnp.dot(a, b, preferred_element_type=F32)


def _dot_nt(a, b):
    return lax.dot_general(a, b, (((1,), (1,)), ((), ())), preferred_element_type=F32)


Q_WIDTH = NSA_HEADS * HEAD_DIM
COL_Q = 0
COL_KV = COL_Q + Q_WIDTH
COL_U = COL_KV + 6 * KV_PAIR
COL_V = COL_U + SGU_WIDTH
COL_MERGE = COL_V + SGU_WIDTH
ROPED_KV = (0, 2, 4)
LOG2_E = 1.4426950408889634


def _inproj_kernel(x_ref, pos_ref, freq_ref, w_ref, lng_ref, lnb_ref, wsp_ref, bsp_ref,
                   q_ref, kc_ref, vc_ref, ks_ref, vs_ref, kw_ref, vw_ref,
                   gates_ref, sgu_ref, merge_ref, *, d_model):
    tm = x_ref.shape[0]
    col_gates = COL_MERGE + 2 * d_model
    xb = x_ref[...].astype(BF16)

    u = jax.nn.gelu(_dot(xb, w_ref[:, COL_U:COL_U + SGU_WIDTH]))
    v = jax.nn.gelu(_dot(xb, w_ref[:, COL_V:COL_V + SGU_WIDTH]))
    v = _layer_norm(v, lng_ref[...], lnb_ref[...]).astype(BF16)

    rm = _dot(xb, w_ref[:, COL_MERGE:COL_MERGE + 2 * d_model])
    merge_ref[...] = jax.nn.sigmoid(rm).astype(merge_ref.dtype)
    rg = _dot(xb, w_ref[:, col_gates:col_gates + LANES])
    gates_ref[0] = jax.nn.sigmoid(rg)

    ang = freq_ref[...] * pos_ref[0]
    cos_f = jnp.cos(ang)
    sin_f = jnp.sin(ang)
    rest = HEAD_DIM - ROPE_DIM
    per_head = lambda parts: jnp.concatenate(parts * (LANES // HEAD_DIM), axis=0).T
    cos_t = per_head([cos_f, cos_f, jnp.ones((rest, tm), F32)])
    sin_lo = per_head([-sin_f, jnp.zeros((ROPE_HALF + rest, tm), F32)])
    sin_hi = per_head([jnp.zeros((ROPE_HALF, tm), F32), sin_f, jnp.zeros((rest, tm), F32)])
    lane = lax.broadcasted_iota(jnp.int32, (1, LANES), 1)

    def rope(t):
        return (t * cos_t + pltpu.roll(t, LANES - ROPE_HALF, axis=1) * sin_lo
                + pltpu.roll(t, ROPE_HALF, axis=1) * sin_hi)

    rq = _dot(xb, w_ref[:, COL_Q:COL_Q + Q_WIDTH])
    scale = HEAD_DIM ** -0.5 * LOG2_E
    for pair in range(NSA_HEADS // 2):
        t = rope(rq[:, pair * LANES:(pair + 1) * LANES]) * scale
        t_sw = pltpu.roll(t, HEAD_DIM, axis=1)
        for half in range(2):
            h = 2 * pair + half
            grp = h // NSA_GROUP_SIZE
            src = t if half == grp else t_sw
            keep = (lane >= HEAD_DIM) if grp == 1 else (lane < HEAD_DIM)
            q_ref[0, h] = jnp.where(keep, src, 0.0).astype(q_ref.dtype)

    rkv = _dot(xb, w_ref[:, COL_KV:COL_KV + 6 * KV_PAIR])
    kv_refs = (kc_ref, vc_ref, ks_ref, vs_ref, kw_ref, vw_ref)
    for i, ref in enumerate(kv_refs):
        t = rkv[:, i * KV_PAIR:(i + 1) * KV_PAIR]
        if i in ROPED_KV:
            t = rope(t)
        ref[0] = t.astype(ref.dtype)

    row = lax.broadcasted_iota(jnp.int32, (SGU_CHUNK, SGU_CHUNK), 0)
    col = lax.broadcasted_iota(jnp.int32, (SGU_CHUNK, SGU_CHUNK), 1)
    gdim = SGU_WIDTH // SGU_GROUPS
    w_sp = [jnp.where(col <= row, wsp_ref[g], 0.0).astype(BF16) for g in range(SGU_GROUPS)]
    for n in range(tm // SGU_CHUNK):
        rows = slice(n * SGU_CHUNK, (n + 1) * SGU_CHUNK)
        mixed = jnp.concatenate(
            [_dot(w_sp[g], v[rows, g * gdim:(g + 1) * gdim]) for g in range(SGU_GROUPS)], axis=1)
        sgu_ref[rows, :] = (u[rows, :] * (mixed + bsp_ref[...])).astype(sgu_ref.dtype)


def _inproj(x, pos_rows, freqs, w_all, ln_g, ln_b, w_sp, b_sp):
    B, S, D = x.shape
    tm = TOKEN_TILE
    nt = S // tm
    wcols = w_all.shape[1]
    x2 = x.reshape(B * S, D)
    kv_shape = jax.ShapeDtypeStruct((B, S, KV_PAIR), BF16)
    kv_spec = pl.BlockSpec((1, tm, KV_PAIR), lambda b, i: (b, i, 0))
    const2 = lambda b, i: (0, 0)
    return pl.pallas_call(
        functools.partial(_inproj_kernel, d_model=D),
        grid=(B, nt),
        in_specs=[
            pl.BlockSpec((tm, D), lambda b, i: (b * nt + i, 0)),
            pl.BlockSpec((1, 1, tm), lambda b, i: (b * nt + i, 0, 0)),
            pl.BlockSpec((ROPE_HALF, 1), const2),
            pl.BlockSpec((D, wcols), const2),
            pl.BlockSpec((1, SGU_WIDTH), const2),
            pl.BlockSpec((1, SGU_WIDTH), const2),
            pl.BlockSpec((SGU_GROUPS, SGU_CHUNK, SGU_CHUNK), lambda b, i: (0, 0, 0)),
            pl.BlockSpec((SGU_CHUNK, SGU_WIDTH), const2),
        ],
        out_specs=[
            pl.BlockSpec((1, NSA_HEADS, tm, LANES), lambda b, i: (b, 0, i, 0)),
            kv_spec, kv_spec, kv_spec, kv_spec, kv_spec, kv_spec,
            pl.BlockSpec((1, tm, LANES), lambda b, i: (b, i, 0)),
            pl.BlockSpec((tm, SGU_WIDTH), lambda b, i: (b * nt + i, 0)),
            pl.BlockSpec((tm, 2 * D), lambda b, i: (b * nt + i, 0)),
        ],
        out_shape=[
            jax.ShapeDtypeStruct((B, NSA_HEADS, S, LANES), BF16),
            kv_shape, kv_shape, kv_shape, kv_shape, kv_shape, kv_shape,
            jax.ShapeDtypeStruct((B, S, LANES), F32),
            jax.ShapeDtypeStruct((B * S, SGU_WIDTH), BF16),
            jax.ShapeDtypeStruct((B * S, 2 * D), BF16),
        ],
        compiler_params=pltpu.CompilerParams(
            dimension_semantics=("parallel", "parallel"), vmem_limit_bytes=VMEM_LIMIT),
        name="inproj",
    )(x2, pos_rows, freqs, w_all, ln_g, ln_b, w_sp, b_sp)


def _compress_kernel(k_ref, v_ref, kpe_ref, kw1_ref, kw2_ref, vpe_ref, vw1_ref, vw2_ref,
                     ko_ref, vo_ref):
    ncp = k_ref.shape[1]

    def one(tok_ref, pe_ref, w1_ref, w2_ref, out_ref):
        a = tok_ref[0]
        width = pe_ref.shape[1]
        pe_first = jnp.broadcast_to(pe_ref[0:1, :], (8, width)).astype(BF16)
        pe_second = jnp.broadcast_to(pe_ref[1:2, :], (8, width)).astype(BF16)
        outs = []
        for g in range(NSA_KV_GROUPS):
            first = _dot(a, w1_ref[0, g])
            second = _dot(a, w1_ref[1, g])
            second = pltpu.roll(second, ncp - 1, axis=0)
            bias = (_dot(pe_first, w1_ref[0, g]) + _dot(pe_second, w1_ref[1, g]))[0:1]
            hid = jax.nn.silu(first + second + bias).astype(BF16)
            outs.append(_dot(hid, w2_ref[...]))
        out_ref[0] = jnp.concatenate(outs, axis=1).astype(out_ref.dtype)

    one(k_ref, kpe_ref, kw1_ref, kw2_ref, ko_ref)
    one(v_ref, vpe_ref, vw1_ref, vw2_ref, vo_ref)


def _compress(k_tok, v_tok, kpe, kw1, kw2, vpe, vw1, vw2):
    B, ncp, width = k_tok.shape
    tok_spec = pl.BlockSpec((1, ncp, width), lambda b: (b, 0, 0))
    pe_spec = pl.BlockSpec(kpe.shape, lambda b: (0, 0))
    w1_spec = pl.BlockSpec(kw1.shape, lambda b: (0, 0, 0, 0))
    w2_spec = pl.BlockSpec(kw2.shape, lambda b: (0, 0))
    out_spec = pl.BlockSpec((1, ncp, KV_PAIR), lambda b: (b, 0, 0))
    out_shape = jax.ShapeDtypeStruct((B, ncp, KV_PAIR), BF16)
    return pl.pallas_call(
        _compress_kernel,
        grid=(B,),
        in_specs=[tok_spec, tok_spec, pe_spec, w1_spec, w2_spec, pe_spec, w1_spec, w2_spec],
        out_specs=[out_spec, out_spec],
        out_shape=[out_shape, out_shape],
        compiler_params=pltpu.CompilerParams(
            dimension_semantics=("parallel",), vmem_limit_bytes=VMEM_LIMIT),
        name="compress",
    )(k_tok, v_tok, kpe, kw1, kw2, vpe, vw1, vw2)


KNOCKED_OUT = -(2.0 ** 127)


def _top_rows(s, row_idx, rounds):
    n = float(s.shape[0])
    row_idx = row_idx.astype(F32)
    for _ in range(rounds):
        m = jnp.max(s, axis=0, keepdims=True)
        first = jnp.min(jnp.where(s == m, row_idx, n), axis=0, keepdims=True)
        s = jnp.where(row_idx == first, KNOCKED_OUT, s)
    return s <= KNOCKED_OUT


def _nsa_kernel(q_ref, kc_ref, vc_ref, ks_ref, vs_ref, kw_ref, vw_ref, gates_ref,
                ovl_ref, onehot_ref, gexp_ref, tok_ref, cbias_ref, wbias_ref, tbias_ref,
                o_ref, lhs_ref, *, n_sel):
    R = NSA_GROUP_SIZE
    QB = Q_BLOCK
    ncp = kc_ref.shape[1]
    ns = ovl_ref.shape[1]
    c = pl.program_id(1)
    start = c * QB
    t_col = start + lax.broadcasted_iota(jnp.int32, (QB, 1), 0)
    lane = lax.broadcasted_iota(jnp.int32, (1, LANES), 1)

    row_valid = jnp.concatenate([t_col >= CMP_BLOCK - 1] * NSA_HEADS, axis=0)

    cur_row = (start + lax.broadcasted_iota(jnp.int32, (1, QB), 1)) // SEL_BLOCK
    j_row = lax.broadcasted_iota(jnp.int32, (ns, QB), 0)
    valid_t = j_row <= cur_row
    forced_t = (j_row == 0) | (j_row == cur_row) | (j_row == cur_row - 1)
    free_t = valid_t & jnp.logical_not(forced_t)

    win_start = pl.multiple_of(jnp.maximum(start - WINDOW, 0), QB)
    first_own_block = start // SEL_BLOCK

    H = NSA_HEADS
    q_all = q_ref[0].reshape(H * QB, LANES)
    ones = jnp.ones((WIN_KEYS, LANES), BF16)
    low = lane < HEAD_DIM

    def with_ones(v):
        return jnp.concatenate([v, ones[:v.shape[0]]], axis=1)

    group_rows = [slice(g * R * QB, (g + 1) * R * QB) for g in range(NSA_KV_GROUPS)]

    def by_group(f):
        return jnp.concatenate([f(rows) for rows in group_rows], axis=0)

    q_tok = jnp.concatenate([q_all, tok_ref[...]], axis=1)

    def masked_scores(keys, key_bias):
        k_aug = jnp.concatenate([keys, key_bias], axis=1)
        return by_group(lambda rows: _dot_nt(q_tok[rows], k_aug))

    cmp_rows = pl.multiple_of(ncp - c * (QB // CMP_STRIDE), 8)
    s_c = masked_scores(kc_ref[0], cbias_ref[pl.ds(cmp_rows, ncp), :].astype(BF16))
    e_c = jnp.exp2(s_c - jnp.max(s_c, axis=-1, keepdims=True))
    inv_c = jnp.where(row_valid, 1.0 / jnp.sum(e_c, axis=-1, keepdims=True), 0.0)
    e_c = e_c.astype(BF16)
    v_ovl = jnp.concatenate([vc_ref[0], ovl_ref[...]], axis=1)
    pv_c = (by_group(lambda rows: _dot(e_c[rows], v_ovl)) * inv_c).reshape(H, QB, LANES + ns)
    o_cmp = pv_c[:, :, :LANES]
    bias = []
    for g in range(NSA_KV_GROUPS):
        imp_t = jnp.sum(pv_c[g * R:(g + 1) * R, :, LANES:], axis=0).T
        picked_t = forced_t | (_top_rows(jnp.where(free_t, imp_t, NEG_INF), j_row,
                                         n_sel - N_FORCED) & valid_t)
        bias_g = jnp.where(picked_t & (j_row < first_own_block), 0.0, NEG_INF).T.astype(BF16)
        bias += [bias_g] * R

    s_wd = masked_scores(
        jnp.concatenate([kw_ref[0, pl.ds(win_start, WIN_KEYS), :], ks_ref[0, pl.ds(start, QB), :]], axis=0),
        jnp.concatenate([wbias_ref[jnp.minimum(c, WINDOW // QB)], tbias_ref[...]], axis=0))
    s_w, s_d = s_wd[:, :WIN_KEYS], s_wd[:, WIN_KEYS:]
    e_w = jnp.exp2(s_w - jnp.max(s_w, axis=-1, keepdims=True))
    e_w = e_w.astype(BF16)
    vw1 = with_ones(vw_ref[0, pl.ds(win_start, WIN_KEYS), :])
    acc_win = by_group(lambda rows: _dot(e_w[rows], vw1))

    lhs_ref[:, :LANES] = q_all
    lhs_ref[:, LANES:] = jnp.concatenate(bias, axis=0)
    m_run = jnp.max(s_d, axis=-1, keepdims=True)
    p_d = jnp.exp2(s_d - m_run).astype(BF16)

    def group_values(v):
        return [jnp.where(low, v, 1.0), jnp.where(low, 1.0, v)]

    v_d = group_values(vs_ref[0, pl.ds(start, QB), :])
    carry = []
    for g, rows in enumerate(group_rows):
        carry += [m_run[rows], _dot(p_d[rows], v_d[g])]

    def sel_body(kb, carry):
        k0 = pl.multiple_of(kb * SEL_KEY_CHUNK, SEL_KEY_CHUNK)
        ke = jnp.concatenate([ks_ref[0, pl.ds(k0, SEL_KEY_CHUNK), :],
                              onehot_ref[pl.ds(k0, SEL_KEY_CHUNK), :]], axis=1)
        s = _dot_nt(lhs_ref[...], ke)
        v = group_values(vs_ref[0, pl.ds(k0, SEL_KEY_CHUNK), :])
        out = []
        for g, rows in enumerate(group_rows):
            m_run, acc = carry[2 * g:2 * g + 2]
            m_new = jnp.maximum(m_run, jnp.max(s[rows], axis=-1, keepdims=True))
            p = jnp.exp2(s[rows] - m_new).astype(BF16)
            out += [m_new, jnp.exp2(m_run - m_new) * acc + _dot(p, v[g])]
        return tuple(out)

    def sel_body_pair(kp, carry):
        return sel_body(2 * kp + 1, sel_body(2 * kp, carry))

    n_chunks = (start + SEL_KEY_CHUNK - 1) // SEL_KEY_CHUNK
    carry = lax.fori_loop(0, n_chunks // 2, sel_body_pair, tuple(carry))
    carry = lax.fori_loop(n_chunks - n_chunks % 2, n_chunks, sel_body, carry)

    gates = gates_ref[0].astype(BF16)
    g_cmp, g_sel, g_win = (_dot(gates, gexp_ref[k]) for k in range(3))

    def pair(x, r):
        x = x.reshape(H, QB, x.shape[-1])
        return jnp.where(low, x[r], x[R + r])

    sel0 = carry[1].reshape(R, QB, LANES)
    sel1 = carry[3].reshape(R, QB, LANES)
    tiles = []
    for r in range(R):
        cols = slice(r * LANES, (r + 1) * LANES)
        sums = pltpu.roll(jnp.where(low, sel1[r], sel0[r]), HEAD_DIM, axis=1)
        o_sel = jnp.where(low, sel0[r], sel1[r]) * (1.0 / sums)
        o_win = pair(acc_win[:, :LANES], r) * (1.0 / pair(acc_win[:, LANES:], r))
        tiles.append(g_cmp[:, cols] * pair(o_cmp, r) + g_sel[:, cols] * o_sel + g_win[:, cols] * o_win)
    o_ref[0] = jnp.concatenate(tiles, axis=1).astype(o_ref.dtype)


def _nsa(q, kc, vc, ks, vs, kw, vw, gates, consts):
    B, H, S, _ = q.shape
    ncp = kc.shape[1]
    ns = consts[0].shape[1]
    nq = S // Q_BLOCK
    whole = lambda n: pl.BlockSpec((1, n, KV_PAIR), lambda b, c: (b, 0, 0))
    return pl.pallas_call(
        functools.partial(_nsa_kernel, n_sel=min(SEL_TOP_N, ns)),
        grid=(B, nq),
        in_specs=[
            pl.BlockSpec((1, H, Q_BLOCK, LANES), lambda b, c: (b, 0, c, 0)),
            whole(ncp), whole(ncp), whole(S), whole(S), whole(S), whole(S),
            pl.BlockSpec((1, Q_BLOCK, LANES), lambda b, c: (b, c, 0)),
        ] + [pl.BlockSpec(a.shape, lambda b, c, nd=a.ndim: (0,) * nd) for a in consts],
        out_specs=pl.BlockSpec((1, Q_BLOCK, H * HEAD_DIM), lambda b, c: (b, c, 0)),
        out_shape=jax.ShapeDtypeStruct((B, S, H * HEAD_DIM), BF16),
        scratch_shapes=[pltpu.VMEM((H * Q_BLOCK, LANES + ns), BF16)],
        compiler_params=pltpu.CompilerParams(
            dimension_semantics=("parallel", "arbitrary"), vmem_limit_bytes=VMEM_LIMIT),
        name="nsa",
    )(q, kc, vc, ks, vs, kw, vw, gates, *consts)


def _merge_kernel(x_ref, nsa_ref, sgu_ref, gate_ref, wbn_ref, wbs_ref, wout_ref, g_ref, b_ref,
                  o_ref, *, alpha):
    d = x_ref.shape[1]
    a = _dot(nsa_ref[...], wbn_ref[...])
    s = _dot(sgu_ref[...], wbs_ref[...])
    merged = gate_ref[:, :d].astype(F32) * a + gate_ref[:, d:].astype(F32) * s
    mix = _dot(merged.astype(BF16), wout_ref[...])
    o_ref[...] = _layer_norm(alpha * x_ref[...] + mix, g_ref[...], b_ref[...])


def _merge(x2, o_nsa, o_sgu, merge_g, wbn, wbs, wout, ln_g, ln_b, alpha):
    T, D = x2.shape
    tm = MERGE_TOKEN_TILE
    row = lambda w: pl.BlockSpec((tm, w), lambda i: (i, 0))
    full = lambda a: pl.BlockSpec(a.shape, lambda i: (0, 0))
    return pl.pallas_call(
        functools.partial(_merge_kernel, alpha=alpha),
        grid=(T // tm,),
        in_specs=[row(D), row(o_nsa.shape[1]), row(o_sgu.shape[1]), row(2 * D),
                  full(wbn), full(wbs), full(wout), full(ln_g), full(ln_b)],
        out_specs=row(D),
        out_shape=jax.ShapeDtypeStruct((T, D), F32),
        compiler_params=pltpu.CompilerParams(
            dimension_semantics=("parallel",), vmem_limit_bytes=VMEM_LIMIT),
        name="merge",
    )(x2, o_nsa, o_sgu, merge_g, wbn, wbs, wout, ln_g, ln_b)


def _ffn_kernel(h_ref, wg_ref, wu_ref, wd_ref, g_ref, b_ref, o_ref, hb_ref, acc_ref, *, alpha):
    j = pl.program_id(1)

    @pl.when(j == 0)
    def _():
        hb_ref[...] = h_ref[...].astype(BF16)
        acc_ref[...] = jnp.zeros_like(acc_ref)

    hb = hb_ref[...]
    act = jax.nn.silu(_dot(hb, wg_ref[...])) * _dot(hb, wu_ref[...])
    acc_ref[...] += _dot(act.astype(BF16), wd_ref[...])

    @pl.when(j == pl.num_programs(1) - 1)
    def _():
        o_ref[...] = _layer_norm(alpha * h_ref[...] + acc_ref[...], g_ref[...], b_ref[...])


def _ffn(h, wg, wu, wd, ln_g, ln_b, alpha):
    T, D = h.shape
    hidden = wg.shape[1]
    tm = FFN_TOKEN_TILE
    th = FFN_HIDDEN_TILE
    n_h = hidden // th
    assert th * n_h == hidden and th % LANES == 0
    return pl.pallas_call(
        functools.partial(_ffn_kernel, alpha=alpha),
        grid=(T // tm, n_h),
        in_specs=[
            pl.BlockSpec((tm, D), lambda i, j: (i, 0)),
            pl.BlockSpec((D, th), lambda i, j: (0, j)),
            pl.BlockSpec((D, th), lambda i, j: (0, j)),
            pl.BlockSpec((th, D), lambda i, j: (j, 0)),
            pl.BlockSpec((1, D), lambda i, j: (0, 0)),
            pl.BlockSpec((1, D), lambda i, j: (0, 0)),
        ],
        out_specs=pl.BlockSpec((tm, D), lambda i, j: (i, 0)),
        out_shape=jax.ShapeDtypeStruct((T, D), F32),
        scratch_shapes=[pltpu.VMEM((tm, D), BF16), pltpu.VMEM((tm, D), F32)],
        compiler_params=pltpu.CompilerParams(
            dimension_semantics=("parallel", "arbitrary"), vmem_limit_bytes=VMEM_LIMIT),
        name="ffn",
    )(h, wg, wu, wd, ln_g, ln_b)


def _split_sizes(d_model):
    return [Q_WIDTH] + [KV_PAIR] * 6 + [3 * NSA_HEADS, SGU_WIDTH, SGU_WIDTH, 2 * d_model]


def _nsa_constants(S):
    ncp = S // CMP_STRIDE
    ns = S // SEL_BLOCK
    QB = Q_BLOCK
    cs = np.arange(ncp)[:, None] * CMP_STRIDE
    ss = np.arange(ns)[None, :] * SEL_BLOCK
    overlap = np.clip(np.minimum(cs + CMP_BLOCK, ss + SEL_BLOCK) - np.maximum(cs, ss), 0, None) / CMP_BLOCK
    onehot = (np.arange(S)[:, None] // SEL_BLOCK) == np.arange(ns)[None, :]
    gate_expand = np.zeros((3, LANES, Q_WIDTH), np.float32)
    for head in range(NSA_HEADS):
        grp, r = divmod(head, NSA_GROUP_SIZE)
        lo = r * LANES + grp * HEAD_DIM
        for k in range(3):
            gate_expand[k, 3 * head + k, lo:lo + HEAD_DIM] = 1.0
    tok = np.tile(np.eye(QB, dtype=np.float32), (NSA_HEADS, 1))
    tl = np.arange(QB)[None, :]
    bias = lambda visible: np.where(visible, 0.0, NEG_INF).astype(np.float32)
    i_rel = np.arange(2 * ncp)[:, None] - ncp
    cmp_bias = bias(CMP_STRIDE * i_rel + CMP_BLOCK - 1 <= tl)
    kk = np.arange(WIN_KEYS)[:, None]
    n_clip = WINDOW // QB
    win_bias = np.stack([bias(kk <= c * QB + tl) for c in range(n_clip)]
                        + [bias((kk <= WINDOW + tl) & (kk > tl))])
    own_bias = bias(np.arange(QB)[:, None] <= tl)
    as_bf16 = lambda a: jnp.asarray(a, dtype=BF16)
    return (as_bf16(overlap), as_bf16(onehot), as_bf16(gate_expand), as_bf16(tok),
            jnp.asarray(cmp_bias), as_bf16(win_bias), as_bf16(own_bias))


def _compress_weights(w1):
    hid = w1.shape[1]
    w = w1.reshape(2, CMP_STRIDE, 1, HEAD_DIM, hid)
    eye = jnp.eye(NSA_KV_GROUPS, dtype=w1.dtype)
    wg = w[:, None] * eye[None, :, None, :, None, None]
    return wg.reshape(2, NSA_KV_GROUPS, CMP_STRIDE * KV_PAIR, hid).astype(BF16)


def _compress_pe(pe):
    p = jnp.broadcast_to(pe.reshape(2, CMP_STRIDE, 1, HEAD_DIM), (2, CMP_STRIDE, NSA_KV_GROUPS, HEAD_DIM))
    return p.reshape(2, CMP_STRIDE * KV_PAIR)


def kernel(x, positions, w_in, pe_ck, w_ck1, w_ck2, pe_cv, w_cv1, w_cv2, ln_sgu_g, ln_sgu_b,
           w_spatial, b_spatial, w_branch_nsa, w_branch_sgu, w_out, ln1_g, ln1_b,
           w_ffn_gate, w_ffn_up, w_ffn_down, ln2_g, ln2_b):
    B, S, D = x.shape
    depth = w_in.shape[0]
    alpha = (2.0 * depth) ** 0.25
    assert S % TOKEN_TILE == 0 and S % SEL_KEY_CHUNK == 0 and S >= WIN_KEYS
    ncp = S // CMP_STRIDE
    ns = S // SEL_BLOCK

    freqs = ROPE_THETA ** (-jnp.arange(ROPE_HALF, dtype=F32) / ROPE_HALF)
    freqs = freqs.reshape(ROPE_HALF, 1)
    pos_rows = positions.astype(F32).reshape(B * S // TOKEN_TILE, 1, TOKEN_TILE)

    nsa_consts = _nsa_constants(S)

    sizes = _split_sizes(D)
    offs = np.concatenate([[0], np.cumsum(sizes)])
    seg = lambda w, i: w[:, offs[i]:offs[i + 1]]

    h = x.reshape(B * S, D)
    for l in range(depth):
        w = w_in[l]
        gate_cols = jnp.pad(seg(w, 7), ((0, 0), (0, LANES - sizes[7])))
        w_all = jnp.concatenate([seg(w, i) for i in (0, 1, 2, 3, 4, 5, 6, 8, 9, 10)] + [gate_cols],
                                axis=1).astype(BF16)
        bsp = jnp.repeat(b_spatial[l].T, SGU_WIDTH // SGU_GROUPS, axis=1)
        q, k_c, v_c, k_s, v_s, k_w, v_w, gates, o_sgu, merge_g = _inproj(
            h.reshape(B, S, D), pos_rows, freqs, w_all,
            ln_sgu_g[l].reshape(1, -1), ln_sgu_b[l].reshape(1, -1), w_spatial[l], bsp)

        kc, vc = _compress(
            k_c.reshape(B, ncp, CMP_STRIDE * KV_PAIR), v_c.reshape(B, ncp, CMP_STRIDE * KV_PAIR),
            _compress_pe(pe_ck[l]), _compress_weights(w_ck1[l]), w_ck2[l].astype(BF16),
            _compress_pe(pe_cv[l]), _compress_weights(w_cv1[l]), w_cv2[l].astype(BF16))

        o_nsa = _nsa(q, kc, vc, k_s, v_s, k_w, v_w, gates, nsa_consts)

        wbn = w_branch_nsa[l].reshape(NSA_KV_GROUPS, NSA_GROUP_SIZE, HEAD_DIM, -1)
        wbn = wbn.transpose(1, 0, 2, 3).reshape(Q_WIDTH, -1)
        h = _merge(h, o_nsa.reshape(B * S, -1), o_sgu, merge_g,
                   wbn.astype(BF16), w_branch_sgu[l].astype(BF16), w_out[l].astype(BF16),
                   ln1_g[l].reshape(1, -1), ln1_b[l].reshape(1, -1), alpha)
        h = _ffn(h, w_ffn_gate[l].astype(BF16), w_ffn_up[l].astype(BF16), w_ffn_down[l].astype(BF16),
                 ln2_g[l].reshape(1, -1), ln2_b[l].reshape(1, -1), alpha)
    return h.reshape(B, S, D)
```

```python
import functools

import numpy as np
import jax
import jax.numpy as jnp
from jax import lax
from jax.experimental import pallas as pl
from jax.experimental.pallas import tpu as pltpu

F32 = jnp.float32
BF16 = jnp.bfloat16

HEAD_DIM = 64
NSA_HEADS = 8
NSA_KV_GROUPS = 2
NSA_GROUP_SIZE = NSA_HEADS // NSA_KV_GROUPS
CMP_BLOCK = 32
CMP_STRIDE = 16
CMP_HIDDEN = 128
SEL_BLOCK = 64
SEL_TOP_N = 16
WINDOW = 512
Q_BLOCK = 128
FORCE_BONUS = 1.0e4
N_FORCED = 3
assert FORCE_BONUS > NSA_GROUP_SIZE and SEL_TOP_N >= N_FORCED
ROPE_THETA = 500000.0
ROPE_DIM = HEAD_DIM // 4
ROPE_HALF = ROPE_DIM // 2
SGU_GROUPS = 4
SGU_CHUNK = 128
SGU_WIDTH = 512
LN_EPS = 1e-5
NEG_INF = -1e30

LANES = 128
KV_PAIR = NSA_KV_GROUPS * HEAD_DIM
assert KV_PAIR == LANES

TOKEN_TILE = 512
MERGE_TOKEN_TILE = 1024
MERGE_SUB_BLOCKS = 4
FFN_TOKEN_TILE = 1024
FFN_SUB_BLOCKS = 4
FFN_HIDDEN_CHUNKS = 2
SEL_KEY_CHUNK = 1024
WIN_KEYS = WINDOW + Q_BLOCK
VMEM_LIMIT = 56 * 1024 * 1024


def _layer_norm(x, g, b):
    mu = jnp.mean(x, axis=-1, keepdims=True)
    xc = x - mu
    var = jnp.mean(xc * xc, axis=-1, keepdims=True)
    return xc * lax.rsqrt(var + LN_EPS) * g + b


def _dot(a, b):
    return jnp.dot(a, b, preferred_element_type=F32)


def _dot_nt(a, b):
    return lax.dot_general(a, b, (((1,), (1,)), ((), ())), preferred_element_type=F32)


Q_WIDTH = NSA_HEADS * HEAD_DIM
COL_Q = 0
COL_KV = COL_Q + Q_WIDTH
COL_U = COL_KV + 6 * KV_PAIR
COL_V = COL_U + SGU_WIDTH
COL_MERGE = COL_V + SGU_WIDTH
ROPED_KV = (0, 2, 4)
LOG2_E = 1.4426950408889634


def _inproj_kernel(x_ref, pos_ref, freq_ref, w_ref, lng_ref, lnb_ref, wsp_ref, bsp_ref,
                   q_ref, kc_ref, vc_ref, ks_ref, vs_ref, kw_ref, vw_ref,
                   gates_ref, sgu_ref, merge_ref, *, d_model):
    tm = x_ref.shape[0]
    col_gates = COL_MERGE + 2 * d_model
    xb = x_ref[...].astype(BF16)

    u = jax.nn.gelu(_dot(xb, w_ref[:, COL_U:COL_U + SGU_WIDTH]))
    v = jax.nn.gelu(_dot(xb, w_ref[:, COL_V:COL_V + SGU_WIDTH]))
    v = _layer_norm(v, lng_ref[...], lnb_ref[...]).astype(BF16)

    rm = _dot(xb, w_ref[:, COL_MERGE:COL_MERGE + 2 * d_model])
    merge_ref[...] = jax.nn.sigmoid(rm).astype(merge_ref.dtype)
    rg = _dot(xb, w_ref[:, col_gates:col_gates + LANES])
    gates_ref[0] = jax.nn.sigmoid(rg)

    ang = freq_ref[...] * pos_ref[0]
    cos_f = jnp.cos(ang)
    sin_f = jnp.sin(ang)
    rest = HEAD_DIM - ROPE_DIM
    per_head = lambda parts: jnp.concatenate(parts * (LANES // HEAD_DIM), axis=0).T
    cos_t = per_head([cos_f, cos_f, jnp.ones((rest, tm), F32)])
    sin_lo = per_head([-sin_f, jnp.zeros((ROPE_HALF + rest, tm), F32)])
    sin_hi = per_head([jnp.zeros((ROPE_HALF, tm), F32), sin_f, jnp.zeros((rest, tm), F32)])
    lane = lax.broadcasted_iota(jnp.int32, (1, LANES), 1)

    def rope(t):
        return (t * cos_t + pltpu.roll(t, LANES - ROPE_HALF, axis=1) * sin_lo
                + pltpu.roll(t, ROPE_HALF, axis=1) * sin_hi)

    rq = _dot(xb, w_ref[:, COL_Q:COL_Q + Q_WIDTH])
    scale = HEAD_DIM ** -0.5 * LOG2_E
    for pair in range(NSA_HEADS // 2):
        t = rope(rq[:, pair * LANES:(pair + 1) * LANES]) * scale
        t_sw = pltpu.roll(t, HEAD_DIM, axis=1)
        for half in range(2):
            h = 2 * pair + half
            grp = h // NSA_GROUP_SIZE
            src = t if half == grp else t_sw
            keep = (lane >= HEAD_DIM) if grp == 1 else (lane < HEAD_DIM)
            q_ref[0, h] = jnp.where(keep, src, 0.0).astype(q_ref.dtype)

    rkv = _dot(xb, w_ref[:, COL_KV:COL_KV + 6 * KV_PAIR])
    kv_refs = (kc_ref, vc_ref, ks_ref, vs_ref, kw_ref, vw_ref)
    for i, ref in enumerate(kv_refs):
        t = rkv[:, i * KV_PAIR:(i + 1) * KV_PAIR]
        if i in ROPED_KV:
            t = rope(t)
        ref[0] = t.astype(ref.dtype)

    row = lax.broadcasted_iota(jnp.int32, (SGU_CHUNK, SGU_CHUNK), 0)
    col = lax.broadcasted_iota(jnp.int32, (SGU_CHUNK, SGU_CHUNK), 1)
    gdim = SGU_WIDTH // SGU_GROUPS
    w_sp = [jnp.where(col <= row, wsp_ref[g], 0.0).astype(BF16) for g in range(SGU_GROUPS)]
    for n in range(tm // SGU_CHUNK):
        rows = slice(n * SGU_CHUNK, (n + 1) * SGU_CHUNK)
        mixed = jnp.concatenate(
            [_dot(w_sp[g], v[rows, g * gdim:(g + 1) * gdim]) for g in range(SGU_GROUPS)], axis=1)
        sgu_ref[rows, :] = (u[rows, :] * (mixed + bsp_ref[...])).astype(sgu_ref.dtype)


def _inproj(x, pos_rows, freqs, w_all, ln_g, ln_b, w_sp, b_sp):
    B, S, D = x.shape
    tm = TOKEN_TILE
    nt = S // tm
    wcols = w_all.shape[1]
    x2 = x.reshape(B * S, D)
    kv_shape = jax.ShapeDtypeStruct((B, S, KV_PAIR), BF16)
    kv_spec = pl.BlockSpec((1, tm, KV_PAIR), lambda b, i: (b, i, 0))
    const2 = lambda b, i: (0, 0)
    return pl.pallas_call(
        functools.partial(_inproj_kernel, d_model=D),
        grid=(B, nt),
        in_specs=[
            pl.BlockSpec((tm, D), lambda b, i: (b * nt + i, 0)),
            pl.BlockSpec((1, 1, tm), lambda b, i: (b * nt + i, 0, 0)),
            pl.BlockSpec((ROPE_HALF, 1), const2),
            pl.BlockSpec((D, wcols), const2),
            pl.BlockSpec((1, SGU_WIDTH), const2),
            pl.BlockSpec((1, SGU_WIDTH), const2),
            pl.BlockSpec((SGU_GROUPS, SGU_CHUNK, SGU_CHUNK), lambda b, i: (0, 0, 0)),
            pl.BlockSpec((SGU_CHUNK, SGU_WIDTH), const2),
        ],
        out_specs=[
            pl.BlockSpec((1, NSA_HEADS, tm, LANES), lambda b, i: (b, 0, i, 0)),
            kv_spec, kv_spec, kv_spec, kv_spec, kv_spec, kv_spec,
            pl.BlockSpec((1, tm, LANES), lambda b, i: (b, i, 0)),
            pl.BlockSpec((tm, SGU_WIDTH), lambda b, i: (b * nt + i, 0)),
            pl.BlockSpec((tm, 2 * D), lambda b, i: (b * nt + i, 0)),
        ],
        out_shape=[
            jax.ShapeDtypeStruct((B, NSA_HEADS, S, LANES), BF16),
            kv_shape, kv_shape, kv_shape, kv_shape, kv_shape, kv_shape,
            jax.ShapeDtypeStruct((B, S, LANES), F32),
            jax.ShapeDtypeStruct((B * S, SGU_WIDTH), BF16),
            jax.ShapeDtypeStruct((B * S, 2 * D), BF16),
        ],
        compiler_params=pltpu.CompilerParams(
            dimension_semantics=("parallel", "parallel"), vmem_limit_bytes=VMEM_LIMIT),
        name="inproj",
    )(x2, pos_rows, freqs, w_all, ln_g, ln_b, w_sp, b_sp)


def _compress_kernel(k_ref, v_ref, kpe_ref, kw1_ref, kw2_ref, vpe_ref, vw1_ref, vw2_ref,
                     ko_ref, vo_ref):
    ncp = k_ref.shape[1]

    def one(tok_ref, pe_ref, w1_ref, w2_ref, out_ref):
        a = tok_ref[0]
        width = pe_ref.shape[1]
        pe_first = jnp.broadcast_to(pe_ref[0:1, :], (8, width)).astype(BF16)
        pe_second = jnp.broadcast_to(pe_ref[1:2, :], (8, width)).astype(BF16)
        outs = []
        for g in range(NSA_KV_GROUPS):
            first = _dot(a, w1_ref[0, g])
            second = _dot(a, w1_ref[1, g])
            second = pltpu.roll(second, ncp - 1, axis=0)
            bias = (_dot(pe_first, w1_ref[0, g]) + _dot(pe_second, w1_ref[1, g]))[0:1]
            hid = jax.nn.silu(first + second + bias).astype(BF16)
            outs.append(_dot(hid, w2_ref[...]))
        out_ref[0] = jnp.concatenate(outs, axis=1).astype(out_ref.dtype)

    one(k_ref, kpe_ref, kw1_ref, kw2_ref, ko_ref)
    one(v_ref, vpe_ref, vw1_ref, vw2_ref, vo_ref)


def _compress(k_tok, v_tok, kpe, kw1, kw2, vpe, vw1, vw2):
    B, ncp, width = k_tok.shape
    tok_spec = pl.BlockSpec((1, ncp, width), lambda b: (b, 0, 0))
    pe_spec = pl.BlockSpec(kpe.shape, lambda b: (0, 0))
    w1_spec = pl.BlockSpec(kw1.shape, lambda b: (0, 0, 0, 0))
    w2_spec = pl.BlockSpec(kw2.shape, lambda b: (0, 0))
    out_spec = pl.BlockSpec((1, ncp, KV_PAIR), lambda b: (b, 0, 0))
    out_shape = jax.ShapeDtypeStruct((B, ncp, KV_PAIR), BF16)
    return pl.pallas_call(
        _compress_kernel,
        grid=(B,),
        in_specs=[tok_spec, tok_spec, pe_spec, w1_spec, w2_spec, pe_spec, w1_spec, w2_spec],
        out_specs=[out_spec, out_spec],
        out_shape=[out_shape, out_shape],
        compiler_params=pltpu.CompilerParams(
            dimension_semantics=("parallel",), vmem_limit_bytes=VMEM_LIMIT),
        name="compress",
    )(k_tok, v_tok, kpe, kw1, kw2, vpe, vw1, vw2)


KNOCKED_OUT = -(2.0 ** 127)


def _top_rows(s, row_idx, rounds):
    n = float(s.shape[0])
    row_idx = row_idx.astype(F32)
    for _ in range(rounds):
        m = jnp.max(s, axis=0, keepdims=True)
        first = jnp.min(jnp.where(s == m, row_idx, n), axis=0, keepdims=True)
        s = jnp.where(row_idx == first, KNOCKED_OUT, s)
    return s <= KNOCKED_OUT


def _nsa_kernel(q_ref, kc_ref, vc_ref, ks_ref, vs_ref, kw_ref, vw_ref, gates_ref,
                ovl_ref, onehot_ref, gexp_ref, tok_ref, cbias_ref, wbias_ref, tbias_ref,
                o_ref, lhs_ref, *, n_sel):
    R = NSA_GROUP_SIZE
    QB = Q_BLOCK
    ncp = kc_ref.shape[1]
    ns = ovl_ref.shape[1]
    c = pl.program_id(1)
    start = c * QB
    t_col = start + lax.broadcasted_iota(jnp.int32, (QB, 1), 0)
    lane = lax.broadcasted_iota(jnp.int32, (1, LANES), 1)

    row_valid = jnp.concatenate([t_col >= CMP_BLOCK - 1] * NSA_HEADS, axis=0)

    cur_row = (start + lax.broadcasted_iota(jnp.int32, (1, QB), 1)) // SEL_BLOCK
    j_row = lax.broadcasted_iota(jnp.int32, (ns, QB), 0)
    valid_t = j_row <= cur_row
    forced_t = (j_row == 0) | (j_row == cur_row) | (j_row == cur_row - 1)
    free_t = valid_t & jnp.logical_not(forced_t)

    win_start = pl.multiple_of(jnp.maximum(start - WINDOW, 0), QB)
    first_own_block = start // SEL_BLOCK

    H = NSA_HEADS
    q_all = q_ref[0].reshape(H * QB, LANES)
    ones = jnp.ones((WIN_KEYS, LANES), BF16)
    low = lane < HEAD_DIM

    def with_ones(v):
        return jnp.concatenate([v, ones[:v.shape[0]]], axis=1)

    group_rows = [slice(g * R * QB, (g + 1) * R * QB) for g in range(NSA_KV_GROUPS)]

    def by_group(f):
        return jnp.concatenate([f(rows) for rows in group_rows], axis=0)

    q_tok = jnp.concatenate([q_all, tok_ref[...]], axis=1)

    def masked_scores(keys, key_bias):
        k_aug = jnp.concatenate([keys, key_bias], axis=1)
        return by_group(lambda rows: _dot_nt(q_tok[rows], k_aug))

    cmp_rows = pl.multiple_of(ncp - c * (QB // CMP_STRIDE), 8)
    s_c = masked_scores(kc_ref[0], cbias_ref[pl.ds(cmp_rows, ncp), :].astype(BF16))
    e_c = jnp.exp2(s_c - jnp.max(s_c, axis=-1, keepdims=True))
    inv_c = jnp.where(row_valid, 1.0 / jnp.sum(e_c, axis=-1, keepdims=True), 0.0)
    e_c = e_c.astype(BF16)
    v_ovl = jnp.concatenate([vc_ref[0], ovl_ref[...]], axis=1)
    pv_c = (by_group(lambda rows: _dot(e_c[rows], v_ovl)) * inv_c).reshape(H, QB, LANES + ns)
    o_cmp = pv_c[:, :, :LANES]
    bias = []
    for g in range(NSA_KV_GROUPS):
        imp_t = jnp.sum(pv_c[g * R:(g + 1) * R, :, LANES:], axis=0).T
        picked_t = forced_t | (_top_rows(jnp.where(free_t, imp_t, NEG_INF), j_row,
                                         n_sel - N_FORCED) & valid_t)
        bias_g = jnp.where(picked_t & (j_row < first_own_block), 0.0, NEG_INF).T.astype(BF16)
        bias += [bias_g] * R

    s_wd = masked_scores(
        jnp.concatenate([kw_ref[0, pl.ds(win_start, WIN_KEYS), :], ks_ref[0, pl.ds(start, QB), :]], axis=0),
        jnp.concatenate([wbias_ref[jnp.minimum(c, WINDOW // QB)], tbias_ref[...]], axis=0))
    s_w, s_d = s_wd[:, :WIN_KEYS], s_wd[:, WIN_KEYS:]
    e_w = jnp.exp2(s_w - jnp.max(s_w, axis=-1, keepdims=True))
    e_w = e_w.astype(BF16)
    vw1 = with_ones(vw_ref[0, pl.ds(win_start, WIN_KEYS), :])
    acc_win = by_group(lambda rows: _dot(e_w[rows], vw1))

    lhs_ref[:, :LANES] = q_all
    lhs_ref[:, LANES:] = jnp.concatenate(bias, axis=0)
    m_run = jnp.max(s_d, axis=-1, keepdims=True)
    p_d = jnp.exp2(s_d - m_run).astype(BF16)

    def group_values(v):
        return [jnp.where(low, v, 1.0), jnp.where(low, 1.0, v)]

    v_d = group_values(vs_ref[0, pl.ds(start, QB), :])
    carry = []
    for g, rows in enumerate(group_rows):
        carry += [m_run[rows], _dot(p_d[rows], v_d[g])]

    def sel_body(kb, carry):
        k0 = pl.multiple_of(kb * SEL_KEY_CHUNK, SEL_KEY_CHUNK)
        ke = jnp.concatenate([ks_ref[0, pl.ds(k0, SEL_KEY_CHUNK), :],
                              onehot_ref[pl.ds(k0, SEL_KEY_CHUNK), :]], axis=1)
        s = _dot_nt(lhs_ref[...], ke)
        v = group_values(vs_ref[0, pl.ds(k0, SEL_KEY_CHUNK), :])
        out = []
        for g, rows in enumerate(group_rows):
            m_run, acc = carry[2 * g:2 * g + 2]
            m_new = jnp.maximum(m_run, jnp.max(s[rows], axis=-1, keepdims=True))
            p = jnp.exp2(s[rows] - m_new).astype(BF16)
            out += [m_new, jnp.exp2(m_run - m_new) * acc + _dot(p, v[g])]
        return tuple(out)

    def sel_body_pair(kp, carry):
        return sel_body(2 * kp + 1, sel_body(2 * kp, carry))

    n_chunks = (start + SEL_KEY_CHUNK - 1) // SEL_KEY_CHUNK
    carry = lax.fori_loop(0, n_chunks // 2, sel_body_pair, tuple(carry))
    carry = lax.fori_loop(n_chunks - n_chunks % 2, n_chunks, sel_body, carry)

    gates = gates_ref[0].astype(BF16)
    g_cmp, g_sel, g_win = (_dot(gates, gexp_ref[k]) for k in range(3))

    def pair(x, r):
        x = x.reshape(H, QB, x.shape[-1])
        return jnp.where(low, x[r], x[R + r])

    sel0 = carry[1].reshape(R, QB, LANES)
    sel1 = carry[3].reshape(R, QB, LANES)
    tiles = []
    for r in range(R):
        cols = slice(r * LANES, (r + 1) * LANES)
        sums = pltpu.roll(jnp.where(low, sel1[r], sel0[r]), HEAD_DIM, axis=1)
        o_sel = jnp.where(low, sel0[r], sel1[r]) * (1.0 / sums)
        o_win = pair(acc_win[:, :LANES], r) * (1.0 / pair(acc_win[:, LANES:], r))
        tiles.append(g_cmp[:, cols] * pair(o_cmp, r) + g_sel[:, cols] * o_sel + g_win[:, cols] * o_win)
    o_ref[0] = jnp.concatenate(tiles, axis=1).astype(o_ref.dtype)


def _nsa(q, kc, vc, ks, vs, kw, vw, gates, consts):
    B, H, S, _ = q.shape
    ncp = kc.shape[1]
    ns = consts[0].shape[1]
    nq = S // Q_BLOCK
    whole = lambda n: pl.BlockSpec((1, n, KV_PAIR), lambda b, c: (b, 0, 0))
    return pl.pallas_call(
        functools.partial(_nsa_kernel, n_sel=min(SEL_TOP_N, ns)),
        grid=(B, nq),
        in_specs=[
            pl.BlockSpec((1, H, Q_BLOCK, LANES), lambda b, c: (b, 0, c, 0)),
            whole(ncp), whole(ncp), whole(S), whole(S), whole(S), whole(S),
            pl.BlockSpec((1, Q_BLOCK, LANES), lambda b, c: (b, c, 0)),
        ] + [pl.BlockSpec(a.shape, lambda b, c, nd=a.ndim: (0,) * nd) for a in consts],
        out_specs=pl.BlockSpec((1, Q_BLOCK, H * HEAD_DIM), lambda b, c: (b, c, 0)),
        out_shape=jax.ShapeDtypeStruct((B, S, H * HEAD_DIM), BF16),
        scratch_shapes=[pltpu.VMEM((H * Q_BLOCK, LANES + ns), BF16)],
        compiler_params=pltpu.CompilerParams(
            dimension_semantics=("parallel", "arbitrary"), vmem_limit_bytes=VMEM_LIMIT),
        name="nsa",
    )(q, kc, vc, ks, vs, kw, vw, gates, *consts)


def _merge_kernel(x_ref, nsa_ref, sgu_ref, gate_ref, wbn_ref, wbs_ref, wout_ref, g_ref, b_ref,
                  o_ref, *, alpha):
    tm, d = x_ref.shape
    sub = tm // MERGE_SUB_BLOCKS
    for i in range(MERGE_SUB_BLOCKS):
        rows = slice(i * sub, (i + 1) * sub)
        a = _dot(nsa_ref[rows, :], wbn_ref[...])
        s = _dot(sgu_ref[rows, :], wbs_ref[...])
        merged = gate_ref[rows, :d].astype(F32) * a + gate_ref[rows, d:].astype(F32) * s
        mix = _dot(merged.astype(BF16), wout_ref[...])
        o_ref[rows, :] = _layer_norm(alpha * x_ref[rows, :] + mix, g_ref[...], b_ref[...])


def _merge(x2, o_nsa, o_sgu, merge_g, wbn, wbs, wout, ln_g, ln_b, alpha):
    T, D = x2.shape
    tm = MERGE_TOKEN_TILE
    row = lambda w: pl.BlockSpec((tm, w), lambda i: (i, 0))
    full = lambda a: pl.BlockSpec(a.shape, lambda i: (0, 0))
    return pl.pallas_call(
        functools.partial(_merge_kernel, alpha=alpha),
        grid=(T // tm,),
        in_specs=[row(D), row(o_nsa.shape[1]), row(o_sgu.shape[1]), row(2 * D),
                  full(wbn), full(wbs), full(wout), full(ln_g), full(ln_b)],
        out_specs=row(D),
        out_shape=jax.ShapeDtypeStruct((T, D), F32),
        compiler_params=pltpu.CompilerParams(
            dimension_semantics=("parallel",), vmem_limit_bytes=VMEM_LIMIT),
        name="merge",
    )(x2, o_nsa, o_sgu, merge_g, wbn, wbs, wout, ln_g, ln_b)


def _ffn_kernel(h_ref, wg_ref, wu_ref, wd_ref, g_ref, b_ref, o_ref, *, alpha):
    tm = h_ref.shape[0]
    hidden = wg_ref.shape[1]
    sub = tm // FFN_SUB_BLOCKS
    th = hidden // FFN_HIDDEN_CHUNKS
    for i in range(FFN_SUB_BLOCKS):
        rows = slice(i * sub, (i + 1) * sub)
        h = h_ref[rows, :]
        hb = h.astype(BF16)
        ffn = None
        for j in range(FFN_HIDDEN_CHUNKS):
            cols = slice(j * th, (j + 1) * th)
            act = jax.nn.silu(_dot(hb, wg_ref[:, cols])) * _dot(hb, wu_ref[:, cols])
            part = _dot(act.astype(BF16), wd_ref[cols, :])
            ffn = part if ffn is None else ffn + part
        o_ref[rows, :] = _layer_norm(alpha * h + ffn, g_ref[...], b_ref[...])


def _ffn(h, wg, wu, wd, ln_g, ln_b, alpha):
    T, D = h.shape
    hidden = wg.shape[1]
    tm = FFN_TOKEN_TILE
    assert (hidden // FFN_HIDDEN_CHUNKS) % LANES == 0
    resident = lambda a: pl.BlockSpec(a.shape, lambda i: (0, 0), pipeline_mode=pl.Buffered(1))
    return pl.pallas_call(
        functools.partial(_ffn_kernel, alpha=alpha),
        grid=(T // tm,),
        in_specs=[pl.BlockSpec((tm, D), lambda i: (i, 0)), resident(wg), resident(wu), resident(wd),
                  resident(ln_g), resident(ln_b)],
        out_specs=pl.BlockSpec((tm, D), lambda i: (i, 0)),
        out_shape=jax.ShapeDtypeStruct((T, D), F32),
        compiler_params=pltpu.CompilerParams(
            dimension_semantics=("parallel",), vmem_limit_bytes=VMEM_LIMIT),
        name="ffn",
    )(h, wg, wu, wd, ln_g, ln_b)


def _split_sizes(d_model):
    return [Q_WIDTH] + [KV_PAIR] * 6 + [3 * NSA_HEADS, SGU_WIDTH, SGU_WIDTH, 2 * d_model]


def _nsa_constants(S):
    ncp = S // CMP_STRIDE
    ns = S // SEL_BLOCK
    QB = Q_BLOCK
    cs = np.arange(ncp)[:, None] * CMP_STRIDE
    ss = np.arange(ns)[None, :] * SEL_BLOCK
    overlap = np.clip(np.minimum(cs + CMP_BLOCK, ss + SEL_BLOCK) - np.maximum(cs, ss), 0, None) / CMP_BLOCK
    onehot = (np.arange(S)[:, None] // SEL_BLOCK) == np.arange(ns)[None, :]
    gate_expand = np.zeros((3, LANES, Q_WIDTH), np.float32)
    for head in range(NSA_HEADS):
        grp, r = divmod(head, NSA_GROUP_SIZE)
        lo = r * LANES + grp * HEAD_DIM
        for k in range(3):
            gate_expand[k, 3 * head + k, lo:lo + HEAD_DIM] = 1.0
    tok = np.tile(np.eye(QB, dtype=np.float32), (NSA_HEADS, 1))
    tl = np.arange(QB)[None, :]
    bias = lambda visible: np.where(visible, 0.0, NEG_INF).astype(np.float32)
    i_rel = np.arange(2 * ncp)[:, None] - ncp
    cmp_bias = bias(CMP_STRIDE * i_rel + CMP_BLOCK - 1 <= tl)
    kk = np.arange(WIN_KEYS)[:, None]
    n_clip = WINDOW // QB
    win_bias = np.stack([bias(kk <= c * QB + tl) for c in range(n_clip)]
                        + [bias((kk <= WINDOW + tl) & (kk > tl))])
    own_bias = bias(np.arange(QB)[:, None] <= tl)
    as_bf16 = lambda a: jnp.asarray(a, dtype=BF16)
    return (as_bf16(overlap), as_bf16(onehot), as_bf16(gate_expand), as_bf16(tok),
            jnp.asarray(cmp_bias), as_bf16(win_bias), as_bf16(own_bias))


def _compress_weights(w1):
    hid = w1.shape[1]
    w = w1.reshape(2, CMP_STRIDE, 1, HEAD_DIM, hid)
    eye = jnp.eye(NSA_KV_GROUPS, dtype=w1.dtype)
    wg = w[:, None] * eye[None, :, None, :, None, None]
    return wg.reshape(2, NSA_KV_GROUPS, CMP_STRIDE * KV_PAIR, hid).astype(BF16)


def _compress_pe(pe):
    p = jnp.broadcast_to(pe.reshape(2, CMP_STRIDE, 1, HEAD_DIM), (2, CMP_STRIDE, NSA_KV_GROUPS, HEAD_DIM))
    return p.reshape(2, CMP_STRIDE * KV_PAIR)


def kernel(x, positions, w_in, pe_ck, w_ck1, w_ck2, pe_cv, w_cv1, w_cv2, ln_sgu_g, ln_sgu_b,
           w_spatial, b_spatial, w_branch_nsa, w_branch_sgu, w_out, ln1_g, ln1_b,
           w_ffn_gate, w_ffn_up, w_ffn_down, ln2_g, ln2_b):
    B, S, D = x.shape
    depth = w_in.shape[0]
    alpha = (2.0 * depth) ** 0.25
    assert S % TOKEN_TILE == 0 and S % SEL_KEY_CHUNK == 0 and S >= WIN_KEYS
    ncp = S // CMP_STRIDE
    ns = S // SEL_BLOCK

    freqs = ROPE_THETA ** (-jnp.arange(ROPE_HALF, dtype=F32) / ROPE_HALF)
    freqs = freqs.reshape(ROPE_HALF, 1)
    pos_rows = positions.astype(F32).reshape(B * S // TOKEN_TILE, 1, TOKEN_TILE)

    nsa_consts = _nsa_constants(S)

    sizes = _split_sizes(D)
    offs = np.concatenate([[0], np.cumsum(sizes)])
    seg = lambda w, i: w[:, offs[i]:offs[i + 1]]

    h = x.reshape(B * S, D)
    for l in range(depth):
        w = w_in[l]
        gate_cols = jnp.pad(seg(w, 7), ((0, 0), (0, LANES - sizes[7])))
        w_all = jnp.concatenate([seg(w, i) for i in (0, 1, 2, 3, 4, 5, 6, 8, 9, 10)] + [gate_cols],
                                axis=1).astype(BF16)
        bsp = jnp.repeat(b_spatial[l].T, SGU_WIDTH // SGU_GROUPS, axis=1)
        q, k_c, v_c, k_s, v_s, k_w, v_w, gates, o_sgu, merge_g = _inproj(
            h.reshape(B, S, D), pos_rows, freqs, w_all,
            ln_sgu_g[l].reshape(1, -1), ln_sgu_b[l].reshape(1, -1), w_spatial[l], bsp)

        kc, vc = _compress(
            k_c.reshape(B, ncp, CMP_STRIDE * KV_PAIR), v_c.reshape(B, ncp, CMP_STRIDE * KV_PAIR),
            _compress_pe(pe_ck[l]), _compress_weights(w_ck1[l]), w_ck2[l].astype(BF16),
            _compress_pe(pe_cv[l]), _compress_weights(w_cv1[l]), w_cv2[l].astype(BF16))

        o_nsa = _nsa(q, kc, vc, k_s, v_s, k_w, v_w, gates, nsa_consts)

        wbn = w_branch_nsa[l].reshape(NSA_KV_GROUPS, NSA_GROUP_SIZE, HEAD_DIM, -1)
        wbn = wbn.transpose(1, 0, 2, 3).reshape(Q_WIDTH, -1)
        h = _merge(h, o_nsa.reshape(B * S, -1), o_sgu, merge_g,
                   wbn.astype(BF16), w_branch_sgu[l].astype(BF16), w_out[l].astype(BF16),
                   ln1_g[l].reshape(1, -1), ln1_b[l].reshape(1, -1), alpha)
        h = _ffn(h, w_ffn_gate[l].astype(BF16), w_ffn_up[l].astype(BF16), w_ffn_down[l].astype(BF16),
                 ln2_g[l].reshape(1, -1), ln2_b[l].reshape(1, -1), alpha)
    return h.reshape(B, S, D)
```

```python
import functools

import numpy as np
import jax
import jax.numpy as jnp
from jax import lax
from jax.experimental import pallas as pl
from jax.experimental.pallas import tpu as pltpu

F32 = jnp.float32
BF16 = jnp.bfloat16

HEAD_DIM = 64
NSA_HEADS = 8
NSA_KV_GROUPS = 2
NSA_GROUP_SIZE = NSA_HEADS // NSA_KV_GROUPS
CMP_BLOCK = 32
CMP_STRIDE = 16
CMP_HIDDEN = 128
SEL_BLOCK = 64
SEL_TOP_N = 16
WINDOW = 512
Q_BLOCK = 128
FORCE_BONUS = 1.0e4
N_FORCED = 3
assert FORCE_BONUS > NSA_GROUP_SIZE and SEL_TOP_N >= N_FORCED
ROPE_THETA = 500000.0
ROPE_DIM = HEAD_DIM // 4
ROPE_HALF = ROPE_DIM // 2
SGU_GROUPS = 4
SGU_CHUNK = 128
SGU_WIDTH = 512
LN_EPS = 1e-5
NEG_INF = -1e30

LANES = 128
KV_PAIR = NSA_KV_GROUPS * HEAD_DIM
assert KV_PAIR == LANES

TOKEN_TILE = 1024
INPROJ_SUB_BLOCKS = 2
MERGE_TOKEN_TILE = 1024
MERGE_SUB_BLOCKS = 4
FFN_TOKEN_TILE = 1024
FFN_SUB_BLOCKS = 4
FFN_HIDDEN_CHUNKS = 2
SEL_KEY_CHUNK = 1024
WIN_KEYS = WINDOW + Q_BLOCK
VMEM_LIMIT = 56 * 1024 * 1024


def _layer_norm(x, g, b):
    mu = jnp.mean(x, axis=-1, keepdims=True)
    xc = x - mu
    var = jnp.mean(xc * xc, axis=-1, keepdims=True)
    return xc * lax.rsqrt(var + LN_EPS) * g + b


def _dot(a, b):
    return jnp.dot(a, b, preferred_element_type=F32)


def _dot_nt(a, b):
    return lax.dot_general(a, b, (((1,), (1,)), ((), ())), preferred_element_type=F32)


Q_WIDTH = NSA_HEADS * HEAD_DIM
COL_Q = 0
COL_KV = COL_Q + Q_WIDTH
COL_U = COL_KV + 6 * KV_PAIR
COL_V = COL_U + SGU_WIDTH
COL_MERGE = COL_V + SGU_WIDTH
ROPED_KV = (0, 2, 4)
LOG2_E = 1.4426950408889634


def _inproj_kernel(x_ref, pos_ref, freq_ref, w_ref, lng_ref, lnb_ref, wsp_ref, bsp_ref,
                   q_ref, kc_ref, vc_ref, ks_ref, vs_ref, kw_ref, vw_ref,
                   gates_ref, sgu_ref, merge_ref, *, d_model):
    col_gates = COL_MERGE + 2 * d_model
    lane = lax.broadcasted_iota(jnp.int32, (1, LANES), 1)
    rest = HEAD_DIM - ROPE_DIM
    scale = HEAD_DIM ** -0.5 * LOG2_E
    kv_refs = (kc_ref, vc_ref, ks_ref, vs_ref, kw_ref, vw_ref)
    row = lax.broadcasted_iota(jnp.int32, (SGU_CHUNK, SGU_CHUNK), 0)
    col = lax.broadcasted_iota(jnp.int32, (SGU_CHUNK, SGU_CHUNK), 1)
    gdim = SGU_WIDTH // SGU_GROUPS
    w_sp = [jnp.where(col <= row, wsp_ref[g], 0.0).astype(BF16) for g in range(SGU_GROUPS)]

    sub = x_ref.shape[0] // INPROJ_SUB_BLOCKS
    for blk in range(INPROJ_SUB_BLOCKS):
        rows = slice(blk * sub, (blk + 1) * sub)
        xb = x_ref[rows, :].astype(BF16)

        u = jax.nn.gelu(_dot(xb, w_ref[:, COL_U:COL_U + SGU_WIDTH]))
        v = jax.nn.gelu(_dot(xb, w_ref[:, COL_V:COL_V + SGU_WIDTH]))
        v = _layer_norm(v, lng_ref[...], lnb_ref[...]).astype(BF16)

        rm = _dot(xb, w_ref[:, COL_MERGE:COL_MERGE + 2 * d_model])
        merge_ref[rows, :] = jax.nn.sigmoid(rm).astype(merge_ref.dtype)
        rg = _dot(xb, w_ref[:, col_gates:col_gates + LANES])
        gates_ref[0, rows, :] = jax.nn.sigmoid(rg)

        ang = freq_ref[...] * pos_ref[0, :, rows]
        cos_f = jnp.cos(ang)
        sin_f = jnp.sin(ang)
        per_head = lambda parts: jnp.concatenate(parts * (LANES // HEAD_DIM), axis=0).T
        cos_t = per_head([cos_f, cos_f, jnp.ones((rest, sub), F32)])
        sin_lo = per_head([-sin_f, jnp.zeros((ROPE_HALF + rest, sub), F32)])
        sin_hi = per_head([jnp.zeros((ROPE_HALF, sub), F32), sin_f, jnp.zeros((rest, sub), F32)])

        def rope(t):
            return (t * cos_t + pltpu.roll(t, LANES - ROPE_HALF, axis=1) * sin_lo
                    + pltpu.roll(t, ROPE_HALF, axis=1) * sin_hi)

        rq = _dot(xb, w_ref[:, COL_Q:COL_Q + Q_WIDTH])
        for pair in range(NSA_HEADS // 2):
            t = rope(rq[:, pair * LANES:(pair + 1) * LANES]) * scale
            t_sw = pltpu.roll(t, HEAD_DIM, axis=1)
            for half in range(2):
                h = 2 * pair + half
                grp = h // NSA_GROUP_SIZE
                src = t if half == grp else t_sw
                keep = (lane >= HEAD_DIM) if grp == 1 else (lane < HEAD_DIM)
                q_ref[0, h, rows, :] = jnp.where(keep, src, 0.0).astype(q_ref.dtype)

        rkv = _dot(xb, w_ref[:, COL_KV:COL_KV + 6 * KV_PAIR])
        for i, ref in enumerate(kv_refs):
            t = rkv[:, i * KV_PAIR:(i + 1) * KV_PAIR]
            if i in ROPED_KV:
                t = rope(t)
            ref[0, rows, :] = t.astype(ref.dtype)

        for n in range(sub // SGU_CHUNK):
            chunk = slice(n * SGU_CHUNK, (n + 1) * SGU_CHUNK)
            mixed = jnp.concatenate(
                [_dot(w_sp[g], v[chunk, g * gdim:(g + 1) * gdim]) for g in range(SGU_GROUPS)], axis=1)
            out_rows = slice(blk * sub + n * SGU_CHUNK, blk * sub + (n + 1) * SGU_CHUNK)
            sgu_ref[out_rows, :] = (u[chunk, :] * (mixed + bsp_ref[...])).astype(sgu_ref.dtype)


def _inproj(x, pos_rows, freqs, w_all, ln_g, ln_b, w_sp, b_sp):
    B, S, D = x.shape
    tm = TOKEN_TILE
    nt = S // tm
    wcols = w_all.shape[1]
    x2 = x.reshape(B * S, D)
    kv_shape = jax.ShapeDtypeStruct((B, S, KV_PAIR), BF16)
    kv_spec = pl.BlockSpec((1, tm, KV_PAIR), lambda b, i: (b, i, 0))
    const2 = lambda b, i: (0, 0)
    return pl.pallas_call(
        functools.partial(_inproj_kernel, d_model=D),
        grid=(B, nt),
        in_specs=[
            pl.BlockSpec((tm, D), lambda b, i: (b * nt + i, 0)),
            pl.BlockSpec((1, 1, tm), lambda b, i: (b * nt + i, 0, 0)),
            pl.BlockSpec((ROPE_HALF, 1), const2),
            pl.BlockSpec((D, wcols), const2, pipeline_mode=pl.Buffered(1)),
            pl.BlockSpec((1, SGU_WIDTH), const2),
            pl.BlockSpec((1, SGU_WIDTH), const2),
            pl.BlockSpec((SGU_GROUPS, SGU_CHUNK, SGU_CHUNK), lambda b, i: (0, 0, 0)),
            pl.BlockSpec((SGU_CHUNK, SGU_WIDTH), const2),
        ],
        out_specs=[
            pl.BlockSpec((1, NSA_HEADS, tm, LANES), lambda b, i: (b, 0, i, 0)),
            kv_spec, kv_spec, kv_spec, kv_spec, kv_spec, kv_spec,
            pl.BlockSpec((1, tm, LANES), lambda b, i: (b, i, 0)),
            pl.BlockSpec((tm, SGU_WIDTH), lambda b, i: (b * nt + i, 0)),
            pl.BlockSpec((tm, 2 * D), lambda b, i: (b * nt + i, 0)),
        ],
        out_shape=[
            jax.ShapeDtypeStruct((B, NSA_HEADS, S, LANES), BF16),
            kv_shape, kv_shape, kv_shape, kv_shape, kv_shape, kv_shape,
            jax.ShapeDtypeStruct((B, S, LANES), F32),
            jax.ShapeDtypeStruct((B * S, SGU_WIDTH), BF16),
            jax.ShapeDtypeStruct((B * S, 2 * D), BF16),
        ],
        compiler_params=pltpu.CompilerParams(
            dimension_semantics=("parallel", "parallel"), vmem_limit_bytes=VMEM_LIMIT),
        name="inproj",
    )(x2, pos_rows, freqs, w_all, ln_g, ln_b, w_sp, b_sp)


def _compress_kernel(k_ref, v_ref, kpe_ref, kw1_ref, kw2_ref, vpe_ref, vw1_ref, vw2_ref,
                     ko_ref, vo_ref):
    ncp = k_ref.shape[1]

    def one(tok_ref, pe_ref, w1_ref, w2_ref, out_ref):
        a = tok_ref[0]
        width = pe_ref.shape[1]
        pe_first = jnp.broadcast_to(pe_ref[0:1, :], (8, width)).astype(BF16)
        pe_second = jnp.broadcast_to(pe_ref[1:2, :], (8, width)).astype(BF16)
        outs = []
        for g in range(NSA_KV_GROUPS):
            first = _dot(a, w1_ref[0, g])
            second = _dot(a, w1_ref[1, g])
            second = pltpu.roll(second, ncp - 1, axis=0)
            bias = (_dot(pe_first, w1_ref[0, g]) + _dot(pe_second, w1_ref[1, g]))[0:1]
            hid = jax.nn.silu(first + second + bias).astype(BF16)
            outs.append(_dot(hid, w2_ref[...]))
        out_ref[0] = jnp.concatenate(outs, axis=1).astype(out_ref.dtype)

    one(k_ref, kpe_ref, kw1_ref, kw2_ref, ko_ref)
    one(v_ref, vpe_ref, vw1_ref, vw2_ref, vo_ref)


def _compress(k_tok, v_tok, kpe, kw1, kw2, vpe, vw1, vw2):
    B, ncp, width = k_tok.shape
    tok_spec = pl.BlockSpec((1, ncp, width), lambda b: (b, 0, 0))
    pe_spec = pl.BlockSpec(kpe.shape, lambda b: (0, 0))
    w1_spec = pl.BlockSpec(kw1.shape, lambda b: (0, 0, 0, 0))
    w2_spec = pl.BlockSpec(kw2.shape, lambda b: (0, 0))
    out_spec = pl.BlockSpec((1, ncp, KV_PAIR), lambda b: (b, 0, 0))
    out_shape = jax.ShapeDtypeStruct((B, ncp, KV_PAIR), BF16)
    return pl.pallas_call(
        _compress_kernel,
        grid=(B,),
        in_specs=[tok_spec, tok_spec, pe_spec, w1_spec, w2_spec, pe_spec, w1_spec, w2_spec],
        out_specs=[out_spec, out_spec],
        out_shape=[out_shape, out_shape],
        compiler_params=pltpu.CompilerParams(
            dimension_semantics=("parallel",), vmem_limit_bytes=VMEM_LIMIT),
        name="compress",
    )(k_tok, v_tok, kpe, kw1, kw2, vpe, vw1, vw2)


KNOCKED_OUT = -(2.0 ** 127)


def _top_rows(s, row_idx, rounds):
    n = float(s.shape[0])
    row_idx = row_idx.astype(F32)
    for _ in range(rounds):
        m = jnp.max(s, axis=0, keepdims=True)
        first = jnp.min(jnp.where(s == m, row_idx, n), axis=0, keepdims=True)
        s = jnp.where(row_idx == first, KNOCKED_OUT, s)
    return s <= KNOCKED_OUT


def _nsa_kernel(q_ref, kc_ref, vc_ref, ks_ref, vs_ref, kw_ref, vw_ref, gates_ref,
                ovl_ref, onehot_ref, gexp_ref, tok_ref, cbias_ref, wbias_ref, tbias_ref,
                o_ref, lhs_ref, *, n_sel):
    R = NSA_GROUP_SIZE
    QB = Q_BLOCK
    ncp = kc_ref.shape[1]
    ns = ovl_ref.shape[1]
    c = pl.program_id(1)
    start = c * QB
    t_col = start + lax.broadcasted_iota(jnp.int32, (QB, 1), 0)
    lane = lax.broadcasted_iota(jnp.int32, (1, LANES), 1)

    row_valid = jnp.concatenate([t_col >= CMP_BLOCK - 1] * NSA_HEADS, axis=0)

    cur_row = (start + lax.broadcasted_iota(jnp.int32, (1, QB), 1)) // SEL_BLOCK
    j_row = lax.broadcasted_iota(jnp.int32, (ns, QB), 0)
    valid_t = j_row <= cur_row
    forced_t = (j_row == 0) | (j_row == cur_row) | (j_row == cur_row - 1)
    free_t = valid_t & jnp.logical_not(forced_t)

    win_start = pl.multiple_of(jnp.maximum(start - WINDOW, 0), QB)
    first_own_block = start // SEL_BLOCK

    H = NSA_HEADS
    q_all = q_ref[0].reshape(H * QB, LANES)
    ones = jnp.ones((WIN_KEYS, LANES), BF16)
    low = lane < HEAD_DIM

    def with_ones(v):
        return jnp.concatenate([v, ones[:v.shape[0]]], axis=1)

    group_rows = [slice(g * R * QB, (g + 1) * R * QB) for g in range(NSA_KV_GROUPS)]

    def by_group(f):
        return jnp.concatenate([f(rows) for rows in group_rows], axis=0)

    q_tok = jnp.concatenate([q_all, tok_ref[...]], axis=1)

    def masked_scores(keys, key_bias):
        k_aug = jnp.concatenate([keys, key_bias], axis=1)
        return by_group(lambda rows: _dot_nt(q_tok[rows], k_aug))

    cmp_rows = pl.multiple_of(ncp - c * (QB // CMP_STRIDE), 8)
    s_c = masked_scores(kc_ref[0], cbias_ref[pl.ds(cmp_rows, ncp), :].astype(BF16))
    e_c = jnp.exp2(s_c - jnp.max(s_c, axis=-1, keepdims=True))
    inv_c = jnp.where(row_valid, 1.0 / jnp.sum(e_c, axis=-1, keepdims=True), 0.0)
    e_c = e_c.astype(BF16)
    v_ovl = jnp.concatenate([vc_ref[0], ovl_ref[...]], axis=1)
    pv_c = (by_group(lambda rows: _dot(e_c[rows], v_ovl)) * inv_c).reshape(H, QB, LANES + ns)
    o_cmp = pv_c[:, :, :LANES]
    bias = []
    for g in range(NSA_KV_GROUPS):
        imp_t = jnp.sum(pv_c[g * R:(g + 1) * R, :, LANES:], axis=0).T
        picked_t = forced_t | (_top_rows(jnp.where(free_t, imp_t, NEG_INF), j_row,
                                         n_sel - N_FORCED) & valid_t)
        bias_g = jnp.where(picked_t & (j_row < first_own_block), 0.0, NEG_INF).T.astype(BF16)
        bias += [bias_g] * R

    s_wd = masked_scores(
        jnp.concatenate([kw_ref[0, pl.ds(win_start, WIN_KEYS), :], ks_ref[0, pl.ds(start, QB), :]], axis=0),
        jnp.concatenate([wbias_ref[jnp.minimum(c, WINDOW // QB)], tbias_ref[...]], axis=0))
    s_w, s_d = s_wd[:, :WIN_KEYS], s_wd[:, WIN_KEYS:]
    e_w = jnp.exp2(s_w - jnp.max(s_w, axis=-1, keepdims=True))
    e_w = e_w.astype(BF16)
    vw1 = with_ones(vw_ref[0, pl.ds(win_start, WIN_KEYS), :])
    acc_win = by_group(lambda rows: _dot(e_w[rows], vw1))

    lhs_ref[:, :LANES] = q_all
    lhs_ref[:, LANES:] = jnp.concatenate(bias, axis=0)
    m_run = jnp.max(s_d, axis=-1, keepdims=True)
    p_d = jnp.exp2(s_d - m_run).astype(BF16)

    def group_values(v):
        return [jnp.where(low, v, 1.0), jnp.where(low, 1.0, v)]

    v_d = group_values(vs_ref[0, pl.ds(start, QB), :])
    carry = []
    for g, rows in enumerate(group_rows):
        carry += [m_run[rows], _dot(p_d[rows], v_d[g])]

    gates = gates_ref[0].astype(BF16)
    g_cmp, g_sel, g_win = (_dot(gates, gexp_ref[k]) for k in range(3))

    def pair(x, r):
        x = x.reshape(H, QB, x.shape[-1])
        return jnp.where(low, x[r], x[R + r])

    tile_cols = [slice(r * LANES, (r + 1) * LANES) for r in range(R)]
    cmp_win = [g_cmp[:, tile_cols[r]] * pair(o_cmp, r)
               + g_win[:, tile_cols[r]] * (pair(acc_win[:, :LANES], r) * (1.0 / pair(acc_win[:, LANES:], r)))
               for r in range(R)]

    def sel_body(kb, carry):
        k0 = pl.multiple_of(kb * SEL_KEY_CHUNK, SEL_KEY_CHUNK)
        ke = jnp.concatenate([ks_ref[0, pl.ds(k0, SEL_KEY_CHUNK), :],
                              onehot_ref[pl.ds(k0, SEL_KEY_CHUNK), :]], axis=1)
        s = _dot_nt(lhs_ref[...], ke)
        v = group_values(vs_ref[0, pl.ds(k0, SEL_KEY_CHUNK), :])
        out = []
        for g, rows in enumerate(group_rows):
            m_run, acc = carry[2 * g:2 * g + 2]
            m_new = jnp.maximum(m_run, jnp.max(s[rows], axis=-1, keepdims=True))
            p = jnp.exp2(s[rows] - m_new).astype(BF16)
            out += [m_new, jnp.exp2(m_run - m_new) * acc + _dot(p, v[g])]
        return tuple(out)

    def sel_body_pair(kp, carry):
        return sel_body(2 * kp + 1, sel_body(2 * kp, carry))

    n_chunks = (start + SEL_KEY_CHUNK - 1) // SEL_KEY_CHUNK
    carry = lax.fori_loop(0, n_chunks // 2, sel_body_pair, tuple(carry))
    carry = lax.fori_loop(n_chunks - n_chunks % 2, n_chunks, sel_body, carry)

    sel0 = carry[1].reshape(R, QB, LANES)
    sel1 = carry[3].reshape(R, QB, LANES)
    tiles = []
    for r in range(R):
        sums = pltpu.roll(jnp.where(low, sel1[r], sel0[r]), HEAD_DIM, axis=1)
        o_sel = jnp.where(low, sel0[r], sel1[r]) * (1.0 / sums)
        tiles.append(cmp_win[r] + g_sel[:, tile_cols[r]] * o_sel)
    o_ref[0] = jnp.concatenate(tiles, axis=1).astype(o_ref.dtype)


def _nsa(q, kc, vc, ks, vs, kw, vw, gates, consts):
    B, H, S, _ = q.shape
    ncp = kc.shape[1]
    ns = consts[0].shape[1]
    nq = S // Q_BLOCK
    whole = lambda n: pl.BlockSpec((1, n, KV_PAIR), lambda b, c: (b, 0, 0))
    return pl.pallas_call(
        functools.partial(_nsa_kernel, n_sel=min(SEL_TOP_N, ns)),
        grid=(B, nq),
        in_specs=[
            pl.BlockSpec((1, H, Q_BLOCK, LANES), lambda b, c: (b, 0, c, 0)),
            whole(ncp), whole(ncp), whole(S), whole(S), whole(S), whole(S),
            pl.BlockSpec((1, Q_BLOCK, LANES), lambda b, c: (b, c, 0)),
        ] + [pl.BlockSpec(a.shape, lambda b, c, nd=a.ndim: (0,) * nd) for a in consts],
        out_specs=pl.BlockSpec((1, Q_BLOCK, H * HEAD_DIM), lambda b, c: (b, c, 0)),
        out_shape=jax.ShapeDtypeStruct((B, S, H * HEAD_DIM), BF16),
        scratch_shapes=[pltpu.VMEM((H * Q_BLOCK, LANES + ns), BF16)],
        compiler_params=pltpu.CompilerParams(
            dimension_semantics=("parallel", "arbitrary"), vmem_limit_bytes=VMEM_LIMIT),
        name="nsa",
    )(q, kc, vc, ks, vs, kw, vw, gates, *consts)


def _merge_kernel(x_ref, nsa_ref, sgu_ref, gate_ref, wbn_ref, wbs_ref, wout_ref, g_ref, b_ref,
                  o_ref, *, alpha):
    tm, d = x_ref.shape
    sub = tm // MERGE_SUB_BLOCKS
    for i in range(MERGE_SUB_BLOCKS):
        rows = slice(i * sub, (i + 1) * sub)
        a = _dot(nsa_ref[rows, :], wbn_ref[...])
        s = _dot(sgu_ref[rows, :], wbs_ref[...])
        merged = gate_ref[rows, :d].astype(F32) * a + gate_ref[rows, d:].astype(F32) * s
        mix = _dot(merged.astype(BF16), wout_ref[...])
        o_ref[rows, :] = _layer_norm(alpha * x_ref[rows, :] + mix, g_ref[...], b_ref[...])


def _merge(x2, o_nsa, o_sgu, merge_g, wbn, wbs, wout, ln_g, ln_b, alpha):
    T, D = x2.shape
    tm = MERGE_TOKEN_TILE
    row = lambda w: pl.BlockSpec((tm, w), lambda i: (i, 0))
    full = lambda a: pl.BlockSpec(a.shape, lambda i: (0, 0))
    return pl.pallas_call(
        functools.partial(_merge_kernel, alpha=alpha),
        grid=(T // tm,),
        in_specs=[row(D), row(o_nsa.shape[1]), row(o_sgu.shape[1]), row(2 * D),
                  full(wbn), full(wbs), full(wout), full(ln_g), full(ln_b)],
        out_specs=row(D),
        out_shape=jax.ShapeDtypeStruct((T, D), F32),
        compiler_params=pltpu.CompilerParams(
            dimension_semantics=("parallel",), vmem_limit_bytes=VMEM_LIMIT),
        name="merge",
    )(x2, o_nsa, o_sgu, merge_g, wbn, wbs, wout, ln_g, ln_b)


def _ffn_kernel(h_ref, wg_ref, wu_ref, wd_ref, g_ref, b_ref, o_ref, *, alpha):
    tm = h_ref.shape[0]
    hidden = wg_ref.shape[1]
    sub = tm // FFN_SUB_BLOCKS
    th = hidden // FFN_HIDDEN_CHUNKS
    for i in range(FFN_SUB_BLOCKS):
        rows = slice(i * sub, (i + 1) * sub)
        h = h_ref[rows, :]
        hb = h.astype(BF16)
        ffn = None
        for j in range(FFN_HIDDEN_CHUNKS):
            cols = slice(j * th, (j + 1) * th)
            act = jax.nn.silu(_dot(hb, wg_ref[:, cols])) * _dot(hb, wu_ref[:, cols])
            part = _dot(act.astype(BF16), wd_ref[cols, :])
            ffn = part if ffn is None else ffn + part
        o_ref[rows, :] = _layer_norm(alpha * h + ffn, g_ref[...], b_ref[...])


def _ffn(h, wg, wu, wd, ln_g, ln_b, alpha):
    T, D = h.shape
    hidden = wg.shape[1]
    tm = FFN_TOKEN_TILE
    assert (hidden // FFN_HIDDEN_CHUNKS) % LANES == 0
    resident = lambda a: pl.BlockSpec(a.shape, lambda i: (0, 0), pipeline_mode=pl.Buffered(1))
    return pl.pallas_call(
        functools.partial(_ffn_kernel, alpha=alpha),
        grid=(T // tm,),
        in_specs=[pl.BlockSpec((tm, D), lambda i: (i, 0)), resident(wg), resident(wu), resident(wd),
                  resident(ln_g), resident(ln_b)],
        out_specs=pl.BlockSpec((tm, D), lambda i: (i, 0)),
        out_shape=jax.ShapeDtypeStruct((T, D), F32),
        compiler_params=pltpu.CompilerParams(
            dimension_semantics=("parallel",), vmem_limit_bytes=VMEM_LIMIT),
        name="ffn",
    )(h, wg, wu, wd, ln_g, ln_b)


def _split_sizes(d_model):
    return [Q_WIDTH] + [KV_PAIR] * 6 + [3 * NSA_HEADS, SGU_WIDTH, SGU_WIDTH, 2 * d_model]


def _nsa_constants(S):
    ncp = S // CMP_STRIDE
    ns = S // SEL_BLOCK
    QB = Q_BLOCK
    cs = np.arange(ncp)[:, None] * CMP_STRIDE
    ss = np.arange(ns)[None, :] * SEL_BLOCK
    overlap = np.clip(np.minimum(cs + CMP_BLOCK, ss + SEL_BLOCK) - np.maximum(cs, ss), 0, None) / CMP_BLOCK
    onehot = (np.arange(S)[:, None] // SEL_BLOCK) == np.arange(ns)[None, :]
    gate_expand = np.zeros((3, LANES, Q_WIDTH), np.float32)
    for head in range(NSA_HEADS):
        grp, r = divmod(head, NSA_GROUP_SIZE)
        lo = r * LANES + grp * HEAD_DIM
        for k in range(3):
            gate_expand[k, 3 * head + k, lo:lo + HEAD_DIM] = 1.0
    tok = np.tile(np.eye(QB, dtype=np.float32), (NSA_HEADS, 1))
    tl = np.arange(QB)[None, :]
    bias = lambda visible: np.where(visible, 0.0, NEG_INF).astype(np.float32)
    i_rel = np.arange(2 * ncp)[:, None] - ncp
    cmp_bias = bias(CMP_STRIDE * i_rel + CMP_BLOCK - 1 <= tl)
    kk = np.arange(WIN_KEYS)[:, None]
    n_clip = WINDOW // QB
    win_bias = np.stack([bias(kk <= c * QB + tl) for c in range(n_clip)]
                        + [bias((kk <= WINDOW + tl) & (kk > tl))])
    own_bias = bias(np.arange(QB)[:, None] <= tl)
    as_bf16 = lambda a: jnp.asarray(a, dtype=BF16)
    return (as_bf16(overlap), as_bf16(onehot), as_bf16(gate_expand), as_bf16(tok),
            jnp.asarray(cmp_bias), as_bf16(win_bias), as_bf16(own_bias))


def _compress_weights(w1):
    hid = w1.shape[1]
    w = w1.reshape(2, CMP_STRIDE, 1, HEAD_DIM, hid)
    eye = jnp.eye(NSA_KV_GROUPS, dtype=w1.dtype)
    wg = w[:, None] * eye[None, :, None, :, None, None]
    return wg.reshape(2, NSA_KV_GROUPS, CMP_STRIDE * KV_PAIR, hid).astype(BF16)


def _compress_pe(pe):
    p = jnp.broadcast_to(pe.reshape(2, CMP_STRIDE, 1, HEAD_DIM), (2, CMP_STRIDE, NSA_KV_GROUPS, HEAD_DIM))
    return p.reshape(2, CMP_STRIDE * KV_PAIR)


def kernel(x, positions, w_in, pe_ck, w_ck1, w_ck2, pe_cv, w_cv1, w_cv2, ln_sgu_g, ln_sgu_b,
           w_spatial, b_spatial, w_branch_nsa, w_branch_sgu, w_out, ln1_g, ln1_b,
           w_ffn_gate, w_ffn_up, w_ffn_down, ln2_g, ln2_b):
    B, S, D = x.shape
    depth = w_in.shape[0]
    alpha = (2.0 * depth) ** 0.25
    assert S % TOKEN_TILE == 0 and S % SEL_KEY_CHUNK == 0 and S >= WIN_KEYS
    ncp = S // CMP_STRIDE
    ns = S // SEL_BLOCK

    freqs = ROPE_THETA ** (-jnp.arange(ROPE_HALF, dtype=F32) / ROPE_HALF)
    freqs = freqs.reshape(ROPE_HALF, 1)
    pos_rows = positions.astype(F32).reshape(B * S // TOKEN_TILE, 1, TOKEN_TILE)

    nsa_consts = _nsa_constants(S)

    sizes = _split_sizes(D)
    offs = np.concatenate([[0], np.cumsum(sizes)])
    seg = lambda w, i: w[:, offs[i]:offs[i + 1]]

    h = x.reshape(B * S, D)
    for l in range(depth):
        w = w_in[l]
        gate_cols = jnp.pad(seg(w, 7), ((0, 0), (0, LANES - sizes[7])))
        w_all = jnp.concatenate([seg(w, i) for i in (0, 1, 2, 3, 4, 5, 6, 8, 9, 10)] + [gate_cols],
                                axis=1).astype(BF16)
        bsp = jnp.repeat(b_spatial[l].T, SGU_WIDTH // SGU_GROUPS, axis=1)
        q, k_c, v_c, k_s, v_s, k_w, v_w, gates, o_sgu, merge_g = _inproj(
            h.reshape(B, S, D), pos_rows, freqs, w_all,
            ln_sgu_g[l].reshape(1, -1), ln_sgu_b[l].reshape(1, -1), w_spatial[l], bsp)

        kc, vc = _compress(
            k_c.reshape(B, ncp, CMP_STRIDE * KV_PAIR), v_c.reshape(B, ncp, CMP_STRIDE * KV_PAIR),
            _compress_pe(pe_ck[l]), _compress_weights(w_ck1[l]), w_ck2[l].astype(BF16),
            _compress_pe(pe_cv[l]), _compress_weights(w_cv1[l]), w_cv2[l].astype(BF16))

        o_nsa = _nsa(q, kc, vc, k_s, v_s, k_w, v_w, gates, nsa_consts)

        wbn = w_branch_nsa[l].reshape(NSA_KV_GROUPS, NSA_GROUP_SIZE, HEAD_DIM, -1)
        wbn = wbn.transpose(1, 0, 2, 3).reshape(Q_WIDTH, -1)
        h = _merge(h, o_nsa.reshape(B * S, -1), o_sgu, merge_g,
                   wbn.astype(BF16), w_branch_sgu[l].astype(BF16), w_out[l].astype(BF16),
                   ln1_g[l].reshape(1, -1), ln1_b[l].reshape(1, -1), alpha)
        h = _ffn(h, w_ffn_gate[l].astype(BF16), w_ffn_up[l].astype(BF16), w_ffn_down[l].astype(BF16),
                 ln2_g[l].reshape(1, -1), ln2_b[l].reshape(1, -1), alpha)
    return h.reshape(B, S, D)
```

```python
import functools

import numpy as np
import jax
import jax.numpy as jnp
from jax import lax
from jax.experimental import pallas as pl
from jax.experimental.pallas import tpu as pltpu

F32 = jnp.float32
BF16 = jnp.bfloat16

HEAD_DIM = 64
NSA_HEADS = 8
NSA_KV_GROUPS = 2
NSA_GROUP_SIZE = NSA_HEADS // NSA_KV_GROUPS
CMP_BLOCK = 32
CMP_STRIDE = 16
CMP_HIDDEN = 128
SEL_BLOCK = 64
SEL_TOP_N = 16
WINDOW = 512
Q_BLOCK = 128
FORCE_BONUS = 1.0e4
N_FORCED = 3
assert FORCE_BONUS > NSA_GROUP_SIZE and SEL_TOP_N >= N_FORCED
ROPE_THETA = 500000.0
ROPE_DIM = HEAD_DIM // 4
ROPE_HALF = ROPE_DIM // 2
SGU_GROUPS = 4
SGU_CHUNK = 128
SGU_WIDTH = 512
LN_EPS = 1e-5
NEG_INF = -1e30

LANES = 128
KV_PAIR = NSA_KV_GROUPS * HEAD_DIM
assert KV_PAIR == LANES

TOKEN_TILE = 1024
INPROJ_SUB_BLOCKS = 2
MERGE_TOKEN_TILE = 1024
MERGE_SUB_BLOCKS = 4
FFN_TOKEN_TILE = 1024
FFN_SUB_BLOCKS = 4
SEL_KEY_CHUNK = 1024
WIN_KEYS = WINDOW + Q_BLOCK
VMEM_LIMIT = 56 * 1024 * 1024


def _layer_norm(x, g, b):
    mu = jnp.mean(x, axis=-1, keepdims=True)
    xc = x - mu
    var = jnp.mean(xc * xc, axis=-1, keepdims=True)
    return xc * lax.rsqrt(var + LN_EPS) * g + b


def _dot(a, b):
    return jnp.dot(a, b, preferred_element_type=F32)


def _dot_nt(a, b):
    return lax.dot_general(a, b, (((1,), (1,)), ((), ())), preferred_element_type=F32)


Q_WIDTH = NSA_HEADS * HEAD_DIM
COL_Q = 0
COL_KV = COL_Q + Q_WIDTH
COL_U = COL_KV + 6 * KV_PAIR
COL_V = COL_U + SGU_WIDTH
COL_MERGE = COL_V + SGU_WIDTH
ROPED_KV = (0, 2, 4)
LOG2_E = 1.4426950408889634


def _inproj_kernel(x_ref, pos_ref, freq_ref, w_ref, lng_ref, lnb_ref, wsp_ref, bsp_ref,
                   q_ref, kc_ref, vc_ref, ks_ref, vs_ref, kw_ref, vw_ref,
                   gates_ref, sgu_ref, merge_ref, *, d_model):
    col_gates = COL_MERGE + 2 * d_model
    lane = lax.broadcasted_iota(jnp.int32, (1, LANES), 1)
    rest = HEAD_DIM - ROPE_DIM
    scale = HEAD_DIM ** -0.5 * LOG2_E
    kv_refs = (kc_ref, vc_ref, ks_ref, vs_ref, kw_ref, vw_ref)
    row = lax.broadcasted_iota(jnp.int32, (SGU_CHUNK, SGU_CHUNK), 0)
    col = lax.broadcasted_iota(jnp.int32, (SGU_CHUNK, SGU_CHUNK), 1)
    gdim = SGU_WIDTH // SGU_GROUPS
    w_sp = [jnp.where(col <= row, wsp_ref[g], 0.0).astype(BF16) for g in range(SGU_GROUPS)]

    sub = x_ref.shape[0] // INPROJ_SUB_BLOCKS
    for blk in range(INPROJ_SUB_BLOCKS):
        rows = slice(blk * sub, (blk + 1) * sub)
        xb = x_ref[rows, :].astype(BF16)

        u = jax.nn.gelu(_dot(xb, w_ref[:, COL_U:COL_U + SGU_WIDTH]))
        v = jax.nn.gelu(_dot(xb, w_ref[:, COL_V:COL_V + SGU_WIDTH]))
        v = _layer_norm(v, lng_ref[...], lnb_ref[...]).astype(BF16)

        rm = _dot(xb, w_ref[:, COL_MERGE:COL_MERGE + 2 * d_model])
        merge_ref[rows, :] = jax.nn.sigmoid(rm).astype(merge_ref.dtype)
        rg = _dot(xb, w_ref[:, col_gates:col_gates + LANES])
        gates_ref[0, rows, :] = jax.nn.sigmoid(rg)

        ang = freq_ref[...] * pos_ref[0, :, rows]
        cos_f = jnp.cos(ang)
        sin_f = jnp.sin(ang)
        per_head = lambda parts: jnp.concatenate(parts * (LANES // HEAD_DIM), axis=0).T
        cos_t = per_head([cos_f, cos_f, jnp.ones((rest, sub), F32)])
        sin_lo = per_head([-sin_f, jnp.zeros((ROPE_HALF + rest, sub), F32)])
        sin_hi = per_head([jnp.zeros((ROPE_HALF, sub), F32), sin_f, jnp.zeros((rest, sub), F32)])

        def rope(t):
            return (t * cos_t + pltpu.roll(t, LANES - ROPE_HALF, axis=1) * sin_lo
                    + pltpu.roll(t, ROPE_HALF, axis=1) * sin_hi)

        rq = _dot(xb, w_ref[:, COL_Q:COL_Q + Q_WIDTH])
        for pair in range(NSA_HEADS // 2):
            t = rope(rq[:, pair * LANES:(pair + 1) * LANES]) * scale
            t_sw = pltpu.roll(t, HEAD_DIM, axis=1)
            for half in range(2):
                h = 2 * pair + half
                grp = h // NSA_GROUP_SIZE
                src = t if half == grp else t_sw
                keep = (lane >= HEAD_DIM) if grp == 1 else (lane < HEAD_DIM)
                q_ref[0, h, rows, :] = jnp.where(keep, src, 0.0).astype(q_ref.dtype)

        rkv = _dot(xb, w_ref[:, COL_KV:COL_KV + 6 * KV_PAIR])
        for i, ref in enumerate(kv_refs):
            t = rkv[:, i * KV_PAIR:(i + 1) * KV_PAIR]
            if i in ROPED_KV:
                t = rope(t)
            ref[0, rows, :] = t.astype(ref.dtype)

        for n in range(sub // SGU_CHUNK):
            chunk = slice(n * SGU_CHUNK, (n + 1) * SGU_CHUNK)
            mixed = jnp.concatenate(
                [_dot(w_sp[g], v[chunk, g * gdim:(g + 1) * gdim]) for g in range(SGU_GROUPS)], axis=1)
            out_rows = slice(blk * sub + n * SGU_CHUNK, blk * sub + (n + 1) * SGU_CHUNK)
            sgu_ref[out_rows, :] = (u[chunk, :] * (mixed + bsp_ref[...])).astype(sgu_ref.dtype)


def _inproj(x, pos_rows, freqs, w_all, ln_g, ln_b, w_sp, b_sp):
    B, S, D = x.shape
    tm = TOKEN_TILE
    nt = S // tm
    wcols = w_all.shape[1]
    x2 = x.reshape(B * S, D)
    kv_shape = jax.ShapeDtypeStruct((B, S, KV_PAIR), BF16)
    kv_spec = pl.BlockSpec((1, tm, KV_PAIR), lambda b, i: (b, i, 0))
    const2 = lambda b, i: (0, 0)
    return pl.pallas_call(
        functools.partial(_inproj_kernel, d_model=D),
        grid=(B, nt),
        in_specs=[
            pl.BlockSpec((tm, D), lambda b, i: (b * nt + i, 0)),
            pl.BlockSpec((1, 1, tm), lambda b, i: (b * nt + i, 0, 0)),
            pl.BlockSpec((ROPE_HALF, 1), const2),
            pl.BlockSpec((D, wcols), const2, pipeline_mode=pl.Buffered(1)),
            pl.BlockSpec((1, SGU_WIDTH), const2),
            pl.BlockSpec((1, SGU_WIDTH), const2),
            pl.BlockSpec((SGU_GROUPS, SGU_CHUNK, SGU_CHUNK), lambda b, i: (0, 0, 0)),
            pl.BlockSpec((SGU_CHUNK, SGU_WIDTH), const2),
        ],
        out_specs=[
            pl.BlockSpec((1, NSA_HEADS, tm, LANES), lambda b, i: (b, 0, i, 0)),
            kv_spec, kv_spec, kv_spec, kv_spec, kv_spec, kv_spec,
            pl.BlockSpec((1, tm, LANES), lambda b, i: (b, i, 0)),
            pl.BlockSpec((tm, SGU_WIDTH), lambda b, i: (b * nt + i, 0)),
            pl.BlockSpec((tm, 2 * D), lambda b, i: (b * nt + i, 0)),
        ],
        out_shape=[
            jax.ShapeDtypeStruct((B, NSA_HEADS, S, LANES), BF16),
            kv_shape, kv_shape, kv_shape, kv_shape, kv_shape, kv_shape,
            jax.ShapeDtypeStruct((B, S, LANES), F32),
            jax.ShapeDtypeStruct((B * S, SGU_WIDTH), BF16),
            jax.ShapeDtypeStruct((B * S, 2 * D), BF16),
        ],
        compiler_params=pltpu.CompilerParams(
            dimension_semantics=("parallel", "parallel"), vmem_limit_bytes=VMEM_LIMIT),
        name="inproj",
    )(x2, pos_rows, freqs, w_all, ln_g, ln_b, w_sp, b_sp)


def _compress_kernel(k_ref, v_ref, kpe_ref, kw1_ref, kw2_ref, vpe_ref, vw1_ref, vw2_ref,
                     ko_ref, vo_ref):
    ncp = k_ref.shape[1]

    def one(tok_ref, pe_ref, w1_ref, w2_ref, out_ref):
        a = tok_ref[0]
        width = pe_ref.shape[1]
        pe_first = jnp.broadcast_to(pe_ref[0:1, :], (8, width)).astype(BF16)
        pe_second = jnp.broadcast_to(pe_ref[1:2, :], (8, width)).astype(BF16)
        outs = []
        for g in range(NSA_KV_GROUPS):
            first = _dot(a, w1_ref[0, g])
            second = _dot(a, w1_ref[1, g])
            second = pltpu.roll(second, ncp - 1, axis=0)
            bias = (_dot(pe_first, w1_ref[0, g]) + _dot(pe_second, w1_ref[1, g]))[0:1]
            hid = jax.nn.silu(first + second + bias).astype(BF16)
            outs.append(_dot(hid, w2_ref[...]))
        out_ref[0] = jnp.concatenate(outs, axis=1).astype(out_ref.dtype)

    one(k_ref, kpe_ref, kw1_ref, kw2_ref, ko_ref)
    one(v_ref, vpe_ref, vw1_ref, vw2_ref, vo_ref)


def _compress(k_tok, v_tok, kpe, kw1, kw2, vpe, vw1, vw2):
    B, ncp, width = k_tok.shape
    tok_spec = pl.BlockSpec((1, ncp, width), lambda b: (b, 0, 0))
    pe_spec = pl.BlockSpec(kpe.shape, lambda b: (0, 0))
    w1_spec = pl.BlockSpec(kw1.shape, lambda b: (0, 0, 0, 0))
    w2_spec = pl.BlockSpec(kw2.shape, lambda b: (0, 0))
    out_spec = pl.BlockSpec((1, ncp, KV_PAIR), lambda b: (b, 0, 0))
    out_shape = jax.ShapeDtypeStruct((B, ncp, KV_PAIR), BF16)
    return pl.pallas_call(
        _compress_kernel,
        grid=(B,),
        in_specs=[tok_spec, tok_spec, pe_spec, w1_spec, w2_spec, pe_spec, w1_spec, w2_spec],
        out_specs=[out_spec, out_spec],
        out_shape=[out_shape, out_shape],
        compiler_params=pltpu.CompilerParams(
            dimension_semantics=("parallel",), vmem_limit_bytes=VMEM_LIMIT),
        name="compress",
    )(k_tok, v_tok, kpe, kw1, kw2, vpe, vw1, vw2)


KNOCKED_OUT = -(2.0 ** 127)


def _top_rows(s, row_idx, rounds):
    n = float(s.shape[0])
    row_idx = row_idx.astype(F32)
    for _ in range(rounds):
        m = jnp.max(s, axis=0, keepdims=True)
        first = jnp.min(jnp.where(s == m, row_idx, n), axis=0, keepdims=True)
        s = jnp.where(row_idx == first, KNOCKED_OUT, s)
    return s <= KNOCKED_OUT


def _nsa_kernel(q_ref, kc_ref, vc_ref, ks_ref, vs_ref, kw_ref, vw_ref, gates_ref,
                ovl_ref, onehot_ref, gexp_ref, tok_ref, cbias_ref, wbias_ref, tbias_ref,
                o_ref, lhs_ref, *, n_sel):
    R = NSA_GROUP_SIZE
    QB = Q_BLOCK
    ncp = kc_ref.shape[1]
    ns = ovl_ref.shape[1]
    c = pl.program_id(1)
    start = c * QB
    t_col = start + lax.broadcasted_iota(jnp.int32, (QB, 1), 0)
    lane = lax.broadcasted_iota(jnp.int32, (1, LANES), 1)

    row_valid = jnp.concatenate([t_col >= CMP_BLOCK - 1] * NSA_HEADS, axis=0)

    cur_row = (start + lax.broadcasted_iota(jnp.int32, (1, QB), 1)) // SEL_BLOCK
    j_row = lax.broadcasted_iota(jnp.int32, (ns, QB), 0)
    valid_t = j_row <= cur_row
    forced_t = (j_row == 0) | (j_row == cur_row) | (j_row == cur_row - 1)
    free_t = valid_t & jnp.logical_not(forced_t)

    win_start = pl.multiple_of(jnp.maximum(start - WINDOW, 0), QB)
    first_own_block = start // SEL_BLOCK

    H = NSA_HEADS
    q_all = q_ref[0].reshape(H * QB, LANES)
    ones = jnp.ones((WIN_KEYS, LANES), BF16)
    low = lane < HEAD_DIM

    def with_ones(v):
        return jnp.concatenate([v, ones[:v.shape[0]]], axis=1)

    group_rows = [slice(g * R * QB, (g + 1) * R * QB) for g in range(NSA_KV_GROUPS)]

    def by_group(f):
        return jnp.concatenate([f(rows) for rows in group_rows], axis=0)

    q_tok = jnp.concatenate([q_all, tok_ref[...]], axis=1)

    def masked_scores(keys, key_bias):
        k_aug = jnp.concatenate([keys, key_bias], axis=1)
        return by_group(lambda rows: _dot_nt(q_tok[rows], k_aug))

    cmp_rows = pl.multiple_of(ncp - c * (QB // CMP_STRIDE), 8)
    s_c = masked_scores(kc_ref[0], cbias_ref[pl.ds(cmp_rows, ncp), :].astype(BF16))
    e_c = jnp.exp2(s_c - jnp.max(s_c, axis=-1, keepdims=True))
    inv_c = jnp.where(row_valid, 1.0 / jnp.sum(e_c, axis=-1, keepdims=True), 0.0)
    e_c = e_c.astype(BF16)
    v_ovl = jnp.concatenate([vc_ref[0], ovl_ref[...]], axis=1)
    pv_c = (by_group(lambda rows: _dot(e_c[rows], v_ovl)) * inv_c).reshape(H, QB, LANES + ns)
    o_cmp = pv_c[:, :, :LANES]
    bias = []
    for g in range(NSA_KV_GROUPS):
        imp_t = jnp.sum(pv_c[g * R:(g + 1) * R, :, LANES:], axis=0).T
        picked_t = forced_t | (_top_rows(jnp.where(free_t, imp_t, NEG_INF), j_row,
                                         n_sel - N_FORCED) & valid_t)
        bias_g = jnp.where(picked_t & (j_row < first_own_block), 0.0, NEG_INF).T.astype(BF16)
        bias += [bias_g] * R

    s_wd = masked_scores(
        jnp.concatenate([kw_ref[0, pl.ds(win_start, WIN_KEYS), :], ks_ref[0, pl.ds(start, QB), :]], axis=0),
        jnp.concatenate([wbias_ref[jnp.minimum(c, WINDOW // QB)], tbias_ref[...]], axis=0))
    s_w, s_d = s_wd[:, :WIN_KEYS], s_wd[:, WIN_KEYS:]
    e_w = jnp.exp2(s_w - jnp.max(s_w, axis=-1, keepdims=True))
    e_w = e_w.astype(BF16)
    vw1 = with_ones(vw_ref[0, pl.ds(win_start, WIN_KEYS), :])
    acc_win = by_group(lambda rows: _dot(e_w[rows], vw1))

    lhs_ref[:, :LANES] = q_all
    lhs_ref[:, LANES:] = jnp.concatenate(bias, axis=0)
    m_run = jnp.max(s_d, axis=-1, keepdims=True)
    p_d = jnp.exp2(s_d - m_run).astype(BF16)

    def group_values(v):
        return [jnp.where(low, v, 1.0), jnp.where(low, 1.0, v)]

    v_d = group_values(vs_ref[0, pl.ds(start, QB), :])
    carry = []
    for g, rows in enumerate(group_rows):
        carry += [m_run[rows], _dot(p_d[rows], v_d[g])]

    gates = gates_ref[0].astype(BF16)
    g_cmp, g_sel, g_win = (_dot(gates, gexp_ref[k]) for k in range(3))

    def pair(x, r):
        x = x.reshape(H, QB, x.shape[-1])
        return jnp.where(low, x[r], x[R + r])

    tile_cols = [slice(r * LANES, (r + 1) * LANES) for r in range(R)]
    cmp_win = [g_cmp[:, tile_cols[r]] * pair(o_cmp, r)
               + g_win[:, tile_cols[r]] * (pair(acc_win[:, :LANES], r) * (1.0 / pair(acc_win[:, LANES:], r)))
               for r in range(R)]

    def sel_body(kb, carry):
        k0 = pl.multiple_of(kb * SEL_KEY_CHUNK, SEL_KEY_CHUNK)
        ke = jnp.concatenate([ks_ref[0, pl.ds(k0, SEL_KEY_CHUNK), :],
                              onehot_ref[pl.ds(k0, SEL_KEY_CHUNK), :]], axis=1)
        s = _dot_nt(lhs_ref[...], ke)
        v = group_values(vs_ref[0, pl.ds(k0, SEL_KEY_CHUNK), :])
        out = []
        for g, rows in enumerate(group_rows):
            m_run, acc = carry[2 * g:2 * g + 2]
            m_new = jnp.maximum(m_run, jnp.max(s[rows], axis=-1, keepdims=True))
            p = jnp.exp2(s[rows] - m_new).astype(BF16)
            out += [m_new, jnp.exp2(m_run - m_new) * acc + _dot(p, v[g])]
        return tuple(out)

    def sel_body_pair(kp, carry):
        return sel_body(2 * kp + 1, sel_body(2 * kp, carry))

    n_chunks = (start + SEL_KEY_CHUNK - 1) // SEL_KEY_CHUNK
    carry = lax.fori_loop(0, n_chunks // 2, sel_body_pair, tuple(carry))
    carry = lax.fori_loop(n_chunks - n_chunks % 2, n_chunks, sel_body, carry)

    sel0 = carry[1].reshape(R, QB, LANES)
    sel1 = carry[3].reshape(R, QB, LANES)
    tiles = []
    for r in range(R):
        sums = pltpu.roll(jnp.where(low, sel1[r], sel0[r]), HEAD_DIM, axis=1)
        o_sel = jnp.where(low, sel0[r], sel1[r]) * (1.0 / sums)
        tiles.append(cmp_win[r] + g_sel[:, tile_cols[r]] * o_sel)
    o_ref[0] = jnp.concatenate(tiles, axis=1).astype(o_ref.dtype)


def _nsa(q, kc, vc, ks, vs, kw, vw, gates, consts):
    B, H, S, _ = q.shape
    ncp = kc.shape[1]
    ns = consts[0].shape[1]
    nq = S // Q_BLOCK
    whole = lambda n: pl.BlockSpec((1, n, KV_PAIR), lambda b, c: (b, 0, 0))
    return pl.pallas_call(
        functools.partial(_nsa_kernel, n_sel=min(SEL_TOP_N, ns)),
        grid=(B, nq),
        in_specs=[
            pl.BlockSpec((1, H, Q_BLOCK, LANES), lambda b, c: (b, 0, c, 0)),
            whole(ncp), whole(ncp), whole(S), whole(S), whole(S), whole(S),
            pl.BlockSpec((1, Q_BLOCK, LANES), lambda b, c: (b, c, 0)),
        ] + [pl.BlockSpec(a.shape, lambda b, c, nd=a.ndim: (0,) * nd) for a in consts],
        out_specs=pl.BlockSpec((1, Q_BLOCK, H * HEAD_DIM), lambda b, c: (b, c, 0)),
        out_shape=jax.ShapeDtypeStruct((B, S, H * HEAD_DIM), BF16),
        scratch_shapes=[pltpu.VMEM((H * Q_BLOCK, LANES + ns), BF16)],
        compiler_params=pltpu.CompilerParams(
            dimension_semantics=("parallel", "arbitrary"), vmem_limit_bytes=VMEM_LIMIT),
        name="nsa",
    )(q, kc, vc, ks, vs, kw, vw, gates, *consts)


def _merge_kernel(x_ref, nsa_ref, sgu_ref, gate_ref, wbn_ref, wbs_ref, wout_ref, g_ref, b_ref,
                  o_ref, *, alpha):
    tm, d = x_ref.shape
    sub = tm // MERGE_SUB_BLOCKS
    for i in range(MERGE_SUB_BLOCKS):
        rows = slice(i * sub, (i + 1) * sub)
        a = _dot(nsa_ref[rows, :], wbn_ref[...])
        s = _dot(sgu_ref[rows, :], wbs_ref[...])
        merged = gate_ref[rows, :d].astype(F32) * a + gate_ref[rows, d:].astype(F32) * s
        mix = _dot(merged.astype(BF16), wout_ref[...])
        o_ref[rows, :] = _layer_norm(alpha * x_ref[rows, :] + mix, g_ref[...], b_ref[...])


def _merge(x2, o_nsa, o_sgu, merge_g, wbn, wbs, wout, ln_g, ln_b, alpha):
    T, D = x2.shape
    tm = MERGE_TOKEN_TILE
    row = lambda w: pl.BlockSpec((tm, w), lambda i: (i, 0))
    full = lambda a: pl.BlockSpec(a.shape, lambda i: (0, 0))
    return pl.pallas_call(
        functools.partial(_merge_kernel, alpha=alpha),
        grid=(T // tm,),
        in_specs=[row(D), row(o_nsa.shape[1]), row(o_sgu.shape[1]), row(2 * D),
                  full(wbn), full(wbs), full(wout), full(ln_g), full(ln_b)],
        out_specs=row(D),
        out_shape=jax.ShapeDtypeStruct((T, D), F32),
        compiler_params=pltpu.CompilerParams(
            dimension_semantics=("parallel",), vmem_limit_bytes=VMEM_LIMIT),
        name="merge",
    )(x2, o_nsa, o_sgu, merge_g, wbn, wbs, wout, ln_g, ln_b)


def _ffn_kernel(h_ref, wg_ref, wu_ref, wd_ref, g_ref, b_ref, o_ref, *, alpha):
    sub = h_ref.shape[0] // FFN_SUB_BLOCKS
    for i in range(FFN_SUB_BLOCKS):
        rows = slice(i * sub, (i + 1) * sub)
        h = h_ref[rows, :]
        hb = h.astype(BF16)
        act = jax.nn.silu(_dot(hb, wg_ref[...])) * _dot(hb, wu_ref[...])
        ffn = _dot(act.astype(BF16), wd_ref[...])
        o_ref[rows, :] = _layer_norm(alpha * h + ffn, g_ref[...], b_ref[...])


def _ffn(h, wg, wu, wd, ln_g, ln_b, alpha):
    T, D = h.shape
    tm = FFN_TOKEN_TILE
    resident = lambda a: pl.BlockSpec(a.shape, lambda i: (0, 0), pipeline_mode=pl.Buffered(1))
    return pl.pallas_call(
        functools.partial(_ffn_kernel, alpha=alpha),
        grid=(T // tm,),
        in_specs=[pl.BlockSpec((tm, D), lambda i: (i, 0)), resident(wg), resident(wu), resident(wd),
                  resident(ln_g), resident(ln_b)],
        out_specs=pl.BlockSpec((tm, D), lambda i: (i, 0)),
        out_shape=jax.ShapeDtypeStruct((T, D), F32),
        compiler_params=pltpu.CompilerParams(
            dimension_semantics=("parallel",), vmem_limit_bytes=VMEM_LIMIT),
        name="ffn",
    )(h, wg, wu, wd, ln_g, ln_b)


def _split_sizes(d_model):
    return [Q_WIDTH] + [KV_PAIR] * 6 + [3 * NSA_HEADS, SGU_WIDTH, SGU_WIDTH, 2 * d_model]


def _nsa_constants(S):
    ncp = S // CMP_STRIDE
    ns = S // SEL_BLOCK
    QB = Q_BLOCK
    cs = np.arange(ncp)[:, None] * CMP_STRIDE
    ss = np.arange(ns)[None, :] * SEL_BLOCK
    overlap = np.clip(np.minimum(cs + CMP_BLOCK, ss + SEL_BLOCK) - np.maximum(cs, ss), 0, None) / CMP_BLOCK
    onehot = (np.arange(S)[:, None] // SEL_BLOCK) == np.arange(ns)[None, :]
    gate_expand = np.zeros((3, LANES, Q_WIDTH), np.float32)
    for head in range(NSA_HEADS):
        grp, r = divmod(head, NSA_GROUP_SIZE)
        lo = r * LANES + grp * HEAD_DIM
        for k in range(3):
            gate_expand[k, 3 * head + k, lo:lo + HEAD_DIM] = 1.0
    tok = np.tile(np.eye(QB, dtype=np.float32), (NSA_HEADS, 1))
    tl = np.arange(QB)[None, :]
    bias = lambda visible: np.where(visible, 0.0, NEG_INF).astype(np.float32)
    i_rel = np.arange(2 * ncp)[:, None] - ncp
    cmp_bias = bias(CMP_STRIDE * i_rel + CMP_BLOCK - 1 <= tl)
    kk = np.arange(WIN_KEYS)[:, None]
    n_clip = WINDOW // QB
    win_bias = np.stack([bias(kk <= c * QB + tl) for c in range(n_clip)]
                        + [bias((kk <= WINDOW + tl) & (kk > tl))])
    own_bias = bias(np.arange(QB)[:, None] <= tl)
    as_bf16 = lambda a: jnp.asarray(a, dtype=BF16)
    return (as_bf16(overlap), as_bf16(onehot), as_bf16(gate_expand), as_bf16(tok),
            jnp.asarray(cmp_bias), as_bf16(win_bias), as_bf16(own_bias))


def _compress_weights(w1):
    hid = w1.shape[1]
    w = w1.reshape(2, CMP_STRIDE, 1, HEAD_DIM, hid)
    eye = jnp.eye(NSA_KV_GROUPS, dtype=w1.dtype)
    wg = w[:, None] * eye[None, :, None, :, None, None]
    return wg.reshape(2, NSA_KV_GROUPS, CMP_STRIDE * KV_PAIR, hid).astype(BF16)


def _compress_pe(pe):
    p = jnp.broadcast_to(pe.reshape(2, CMP_STRIDE, 1, HEAD_DIM), (2, CMP_STRIDE, NSA_KV_GROUPS, HEAD_DIM))
    return p.reshape(2, CMP_STRIDE * KV_PAIR)


def kernel(x, positions, w_in, pe_ck, w_ck1, w_ck2, pe_cv, w_cv1, w_cv2, ln_sgu_g, ln_sgu_b,
           w_spatial, b_spatial, w_branch_nsa, w_branch_sgu, w_out, ln1_g, ln1_b,
           w_ffn_gate, w_ffn_up, w_ffn_down, ln2_g, ln2_b):
    B, S, D = x.shape
    depth = w_in.shape[0]
    alpha = (2.0 * depth) ** 0.25
    assert S % TOKEN_TILE == 0 and S % SEL_KEY_CHUNK == 0 and S >= WIN_KEYS
    ncp = S // CMP_STRIDE
    ns = S // SEL_BLOCK

    freqs = ROPE_THETA ** (-jnp.arange(ROPE_HALF, dtype=F32) / ROPE_HALF)
    freqs = freqs.reshape(ROPE_HALF, 1)
    pos_rows = positions.astype(F32).reshape(B * S // TOKEN_TILE, 1, TOKEN_TILE)

    nsa_consts = _nsa_constants(S)

    sizes = _split_sizes(D)
    offs = np.concatenate([[0], np.cumsum(sizes)])
    seg = lambda w, i: w[:, offs[i]:offs[i + 1]]

    h = x.reshape(B * S, D)
    for l in range(depth):
        w = w_in[l]
        gate_cols = jnp.pad(seg(w, 7), ((0, 0), (0, LANES - sizes[7])))
        w_all = jnp.concatenate([seg(w, i) for i in (0, 1, 2, 3, 4, 5, 6, 8, 9, 10)] + [gate_cols],
                                axis=1).astype(BF16)
        bsp = jnp.repeat(b_spatial[l].T, SGU_WIDTH // SGU_GROUPS, axis=1)
        q, k_c, v_c, k_s, v_s, k_w, v_w, gates, o_sgu, merge_g = _inproj(
            h.reshape(B, S, D), pos_rows, freqs, w_all,
            ln_sgu_g[l].reshape(1, -1), ln_sgu_b[l].reshape(1, -1), w_spatial[l], bsp)

        kc, vc = _compress(
            k_c.reshape(B, ncp, CMP_STRIDE * KV_PAIR), v_c.reshape(B, ncp, CMP_STRIDE * KV_PAIR),
            _compress_pe(pe_ck[l]), _compress_weights(w_ck1[l]), w_ck2[l].astype(BF16),
            _compress_pe(pe_cv[l]), _compress_weights(w_cv1[l]), w_cv2[l].astype(BF16))

        o_nsa = _nsa(q, kc, vc, k_s, v_s, k_w, v_w, gates, nsa_consts)

        wbn = w_branch_nsa[l].reshape(NSA_KV_GROUPS, NSA_GROUP_SIZE, HEAD_DIM, -1)
        wbn = wbn.transpose(1, 0, 2, 3).reshape(Q_WIDTH, -1)
        h = _merge(h, o_nsa.reshape(B * S, -1), o_sgu, merge_g,
                   wbn.astype(BF16), w_branch_sgu[l].astype(BF16), w_out[l].astype(BF16),
                   ln1_g[l].reshape(1, -1), ln1_b[l].reshape(1, -1), alpha)
        h = _ffn(h, w_ffn_gate[l].astype(BF16), w_ffn_up[l].astype(BF16), w_ffn_down[l].astype(BF16),
                 ln2_g[l].reshape(1, -1), ln2_b[l].reshape(1, -1), alpha)
    return h.reshape(B, S, D)
```

```python
import functools

import numpy as np
import jax
import jax.numpy as jnp
from jax import lax
from jax.experimental import pallas as pl
from jax.experimental.pallas import tpu as pltpu

F32 = jnp.float32
BF16 = jnp.bfloat16

HEAD_DIM = 64
NSA_HEADS = 8
NSA_KV_GROUPS = 2
NSA_GROUP_SIZE = NSA_HEADS // NSA_KV_GROUPS
CMP_BLOCK = 32
CMP_STRIDE = 16
CMP_HIDDEN = 128
SEL_BLOCK = 64
SEL_TOP_N = 16
WINDOW = 512
Q_BLOCK = 128
FORCE_BONUS = 1.0e4
N_FORCED = 3
assert FORCE_BONUS > NSA_GROUP_SIZE and SEL_TOP_N >= N_FORCED
ROPE_THETA = 500000.0
ROPE_DIM = HEAD_DIM // 4
ROPE_HALF = ROPE_DIM // 2
SGU_GROUPS = 4
SGU_CHUNK = 128
SGU_WIDTH = 512
LN_EPS = 1e-5
NEG_INF = -1e30

LANES = 128
KV_PAIR = NSA_KV_GROUPS * HEAD_DIM
assert KV_PAIR == LANES

TOKEN_TILE = 1024
INPROJ_SUB_BLOCKS = 2
MERGE_TOKEN_TILE = 1024
MERGE_SUB_BLOCKS = 4
FFN_TOKEN_TILE = 1024
FFN_SUB_BLOCKS = 4
SEL_KEY_CHUNK = 1024
WIN_KEYS = WINDOW + Q_BLOCK
VMEM_LIMIT = 56 * 1024 * 1024


def _layer_norm(x, g, b):
    mu = jnp.mean(x, axis=-1, keepdims=True)
    xc = x - mu
    var = jnp.mean(xc * xc, axis=-1, keepdims=True)
    return xc * lax.rsqrt(var + LN_EPS) * g + b


def _dot(a, b):
    return jnp.dot(a, b, preferred_element_type=F32)


def _dot_nt(a, b):
    return lax.dot_general(a, b, (((1,), (1,)), ((), ())), preferred_element_type=F32)


Q_WIDTH = NSA_HEADS * HEAD_DIM
COL_Q = 0
COL_KV = COL_Q + Q_WIDTH
COL_U = COL_KV + 6 * KV_PAIR
COL_V = COL_U + SGU_WIDTH
COL_MERGE = COL_V + SGU_WIDTH
ROPED_KV = (0, 2, 4)
COMPRESSED_KV = (0, 1)
LOG2_E = 1.4426950408889634


def _inproj_kernel(x_ref, pos_ref, freq_ref, w_ref, lng_ref, lnb_ref, wsp_ref, bsp_ref,
                   q_ref, kc_ref, vc_ref, ks_ref, vs_ref, kw_ref, vw_ref,
                   gates_ref, sgu_ref, merge_ref, stage_ref, *, d_model):
    col_gates = COL_MERGE + 2 * d_model
    lane = lax.broadcasted_iota(jnp.int32, (1, LANES), 1)
    rest = HEAD_DIM - ROPE_DIM
    scale = HEAD_DIM ** -0.5 * LOG2_E
    kv_refs = (kc_ref, vc_ref, ks_ref, vs_ref, kw_ref, vw_ref)
    row = lax.broadcasted_iota(jnp.int32, (SGU_CHUNK, SGU_CHUNK), 0)
    col = lax.broadcasted_iota(jnp.int32, (SGU_CHUNK, SGU_CHUNK), 1)
    gdim = SGU_WIDTH // SGU_GROUPS
    w_sp = [jnp.where(col <= row, wsp_ref[g], 0.0).astype(BF16) for g in range(SGU_GROUPS)]

    sub = x_ref.shape[0] // INPROJ_SUB_BLOCKS
    for blk in range(INPROJ_SUB_BLOCKS):
        rows = slice(blk * sub, (blk + 1) * sub)
        xb = x_ref[rows, :].astype(BF16)

        u = jax.nn.gelu(_dot(xb, w_ref[:, COL_U:COL_U + SGU_WIDTH]))
        v = jax.nn.gelu(_dot(xb, w_ref[:, COL_V:COL_V + SGU_WIDTH]))
        v = _layer_norm(v, lng_ref[...], lnb_ref[...]).astype(BF16)

        rm = _dot(xb, w_ref[:, COL_MERGE:COL_MERGE + 2 * d_model])
        merge_ref[rows, :] = jax.nn.sigmoid(rm).astype(merge_ref.dtype)
        rg = _dot(xb, w_ref[:, col_gates:col_gates + LANES])
        gates_ref[0, rows, :] = jax.nn.sigmoid(rg)

        ang = freq_ref[...] * pos_ref[0, :, rows]
        cos_f = jnp.cos(ang)
        sin_f = jnp.sin(ang)
        per_head = lambda parts: jnp.concatenate(parts * (LANES // HEAD_DIM), axis=0).T
        cos_t = per_head([cos_f, cos_f, jnp.ones((rest, sub), F32)])
        sin_lo = per_head([-sin_f, jnp.zeros((ROPE_HALF + rest, sub), F32)])
        sin_hi = per_head([jnp.zeros((ROPE_HALF, sub), F32), sin_f, jnp.zeros((rest, sub), F32)])

        def rope(t):
            return (t * cos_t + pltpu.roll(t, LANES - ROPE_HALF, axis=1) * sin_lo
                    + pltpu.roll(t, ROPE_HALF, axis=1) * sin_hi)

        rq = _dot(xb, w_ref[:, COL_Q:COL_Q + Q_WIDTH])
        for pair in range(NSA_HEADS // 2):
            t = rope(rq[:, pair * LANES:(pair + 1) * LANES]) * scale
            t_sw = pltpu.roll(t, HEAD_DIM, axis=1)
            for half in range(2):
                h = 2 * pair + half
                grp = h // NSA_GROUP_SIZE
                src = t if half == grp else t_sw
                keep = (lane >= HEAD_DIM) if grp == 1 else (lane < HEAD_DIM)
                q_ref[0, h, rows, :] = jnp.where(keep, src, 0.0).astype(q_ref.dtype)

        rkv = _dot(xb, w_ref[:, COL_KV:COL_KV + 6 * KV_PAIR])
        for i, ref in enumerate(kv_refs):
            t = rkv[:, i * KV_PAIR:(i + 1) * KV_PAIR]
            if i in ROPED_KV:
                t = rope(t)
            if i in COMPRESSED_KV:
                stage_ref[i] = t
                n_rows = sub // CMP_STRIDE
                slabs = [stage_ref[i, pl.ds(tok, n_rows, stride=CMP_STRIDE), :] for tok in range(CMP_STRIDE)]
                ref[0, blk * n_rows:(blk + 1) * n_rows, :] = jnp.concatenate(slabs, axis=1).astype(ref.dtype)
            else:
                ref[0, rows, :] = t.astype(ref.dtype)

        for n in range(sub // SGU_CHUNK):
            chunk = slice(n * SGU_CHUNK, (n + 1) * SGU_CHUNK)
            mixed = jnp.concatenate(
                [_dot(w_sp[g], v[chunk, g * gdim:(g + 1) * gdim]) for g in range(SGU_GROUPS)], axis=1)
            out_rows = slice(blk * sub + n * SGU_CHUNK, blk * sub + (n + 1) * SGU_CHUNK)
            sgu_ref[out_rows, :] = (u[chunk, :] * (mixed + bsp_ref[...])).astype(sgu_ref.dtype)


def _inproj(x, pos_rows, freqs, w_all, ln_g, ln_b, w_sp, b_sp):
    B, S, D = x.shape
    tm = TOKEN_TILE
    nt = S // tm
    wcols = w_all.shape[1]
    x2 = x.reshape(B * S, D)
    kv_shape = jax.ShapeDtypeStruct((B, S, KV_PAIR), BF16)
    kv_spec = pl.BlockSpec((1, tm, KV_PAIR), lambda b, i: (b, i, 0))
    cmp_shape = jax.ShapeDtypeStruct((B, S // CMP_STRIDE, CMP_STRIDE * KV_PAIR), BF16)
    cmp_spec = pl.BlockSpec((1, tm // CMP_STRIDE, CMP_STRIDE * KV_PAIR), lambda b, i: (b, i, 0))
    const2 = lambda b, i: (0, 0)
    return pl.pallas_call(
        functools.partial(_inproj_kernel, d_model=D),
        grid=(B, nt),
        in_specs=[
            pl.BlockSpec((tm, D), lambda b, i: (b * nt + i, 0)),
            pl.BlockSpec((1, 1, tm), lambda b, i: (b * nt + i, 0, 0)),
            pl.BlockSpec((ROPE_HALF, 1), const2),
            pl.BlockSpec((D, wcols), const2, pipeline_mode=pl.Buffered(1)),
            pl.BlockSpec((1, SGU_WIDTH), const2),
            pl.BlockSpec((1, SGU_WIDTH), const2),
            pl.BlockSpec((SGU_GROUPS, SGU_CHUNK, SGU_CHUNK), lambda b, i: (0, 0, 0)),
            pl.BlockSpec((SGU_CHUNK, SGU_WIDTH), const2),
        ],
        out_specs=[
            pl.BlockSpec((1, NSA_HEADS, tm, LANES), lambda b, i: (b, 0, i, 0)),
            cmp_spec, cmp_spec, kv_spec, kv_spec, kv_spec, kv_spec,
            pl.BlockSpec((1, tm, LANES), lambda b, i: (b, i, 0)),
            pl.BlockSpec((tm, SGU_WIDTH), lambda b, i: (b * nt + i, 0)),
            pl.BlockSpec((tm, 2 * D), lambda b, i: (b * nt + i, 0)),
        ],
        out_shape=[
            jax.ShapeDtypeStruct((B, NSA_HEADS, S, LANES), BF16),
            cmp_shape, cmp_shape, kv_shape, kv_shape, kv_shape, kv_shape,
            jax.ShapeDtypeStruct((B, S, LANES), F32),
            jax.ShapeDtypeStruct((B * S, SGU_WIDTH), BF16),
            jax.ShapeDtypeStruct((B * S, 2 * D), BF16),
        ],
        scratch_shapes=[pltpu.VMEM((len(COMPRESSED_KV), tm // INPROJ_SUB_BLOCKS, KV_PAIR), F32)],
        compiler_params=pltpu.CompilerParams(
            dimension_semantics=("parallel", "parallel"), vmem_limit_bytes=VMEM_LIMIT),
        name="inproj",
    )(x2, pos_rows, freqs, w_all, ln_g, ln_b, w_sp, b_sp)


def _compress_kernel(k_ref, v_ref, kpe_ref, kw1_ref, kw2_ref, vpe_ref, vw1_ref, vw2_ref,
                     ko_ref, vo_ref):
    ncp = k_ref.shape[1]

    def one(tok_ref, pe_ref, w1_ref, w2_ref, out_ref):
        a = tok_ref[0]
        width = pe_ref.shape[1]
        pe_first = jnp.broadcast_to(pe_ref[0:1, :], (8, width)).astype(BF16)
        pe_second = jnp.broadcast_to(pe_ref[1:2, :], (8, width)).astype(BF16)
        outs = []
        for g in range(NSA_KV_GROUPS):
            first = _dot(a, w1_ref[0, g])
            second = _dot(a, w1_ref[1, g])
            second = pltpu.roll(second, ncp - 1, axis=0)
            bias = (_dot(pe_first, w1_ref[0, g]) + _dot(pe_second, w1_ref[1, g]))[0:1]
            hid = jax.nn.silu(first + second + bias).astype(BF16)
            outs.append(_dot(hid, w2_ref[...]))
        out_ref[0] = jnp.concatenate(outs, axis=1).astype(out_ref.dtype)

    one(k_ref, kpe_ref, kw1_ref, kw2_ref, ko_ref)
    one(v_ref, vpe_ref, vw1_ref, vw2_ref, vo_ref)


def _compress(k_tok, v_tok, kpe, kw1, kw2, vpe, vw1, vw2):
    B, ncp, width = k_tok.shape
    tok_spec = pl.BlockSpec((1, ncp, width), lambda b: (b, 0, 0))
    pe_spec = pl.BlockSpec(kpe.shape, lambda b: (0, 0))
    w1_spec = pl.BlockSpec(kw1.shape, lambda b: (0, 0, 0, 0))
    w2_spec = pl.BlockSpec(kw2.shape, lambda b: (0, 0))
    out_spec = pl.BlockSpec((1, ncp, KV_PAIR), lambda b: (b, 0, 0))
    out_shape = jax.ShapeDtypeStruct((B, ncp, KV_PAIR), BF16)
    return pl.pallas_call(
        _compress_kernel,
        grid=(B,),
        in_specs=[tok_spec, tok_spec, pe_spec, w1_spec, w2_spec, pe_spec, w1_spec, w2_spec],
        out_specs=[out_spec, out_spec],
        out_shape=[out_shape, out_shape],
        compiler_params=pltpu.CompilerParams(
            dimension_semantics=("parallel",), vmem_limit_bytes=VMEM_LIMIT),
        name="compress",
    )(k_tok, v_tok, kpe, kw1, kw2, vpe, vw1, vw2)


KNOCKED_OUT = -(2.0 ** 127)


def _top_rows(s, row_idx, rounds):
    n = float(s.shape[0])
    row_idx = row_idx.astype(F32)
    for _ in range(rounds):
        m = jnp.max(s, axis=0, keepdims=True)
        first = jnp.min(jnp.where(s == m, row_idx, n), axis=0, keepdims=True)
        s = jnp.where(row_idx == first, KNOCKED_OUT, s)
    return s <= KNOCKED_OUT


def _nsa_kernel(q_ref, kc_ref, vc_ref, ks_ref, vs_ref, kw_ref, vw_ref, gates_ref,
                ovl_ref, onehot_ref, gexp_ref, tok_ref, cbias_ref, wbias_ref, tbias_ref,
                o_ref, lhs_ref, *, n_sel):
    R = NSA_GROUP_SIZE
    QB = Q_BLOCK
    ncp = kc_ref.shape[1]
    ns = ovl_ref.shape[1]
    c = pl.program_id(1)
    start = c * QB
    t_col = start + lax.broadcasted_iota(jnp.int32, (QB, 1), 0)
    lane = lax.broadcasted_iota(jnp.int32, (1, LANES), 1)

    row_valid = jnp.concatenate([t_col >= CMP_BLOCK - 1] * NSA_HEADS, axis=0)

    cur_row = (start + lax.broadcasted_iota(jnp.int32, (1, QB), 1)) // SEL_BLOCK
    j_row = lax.broadcasted_iota(jnp.int32, (ns, QB), 0)
    valid_t = j_row <= cur_row
    forced_t = (j_row == 0) | (j_row == cur_row) | (j_row == cur_row - 1)
    free_t = valid_t & jnp.logical_not(forced_t)

    win_start = pl.multiple_of(jnp.maximum(start - WINDOW, 0), QB)
    first_own_block = start // SEL_BLOCK

    H = NSA_HEADS
    q_all = q_ref[0].reshape(H * QB, LANES)
    ones = jnp.ones((WIN_KEYS, LANES), BF16)
    low = lane < HEAD_DIM

    def with_ones(v):
        return jnp.concatenate([v, ones[:v.shape[0]]], axis=1)

    group_rows = [slice(g * R * QB, (g + 1) * R * QB) for g in range(NSA_KV_GROUPS)]

    def by_group(f):
        return jnp.concatenate([f(rows) for rows in group_rows], axis=0)

    q_tok = jnp.concatenate([q_all, tok_ref[...]], axis=1)

    def masked_scores(keys, key_bias):
        k_aug = jnp.concatenate([keys, key_bias], axis=1)
        return by_group(lambda rows: _dot_nt(q_tok[rows], k_aug))

    cmp_rows = pl.multiple_of(ncp - c * (QB // CMP_STRIDE), 8)
    s_c = masked_scores(kc_ref[0], cbias_ref[pl.ds(cmp_rows, ncp), :].astype(BF16))
    e_c = jnp.exp2(s_c - jnp.max(s_c, axis=-1, keepdims=True))
    inv_c = jnp.where(row_valid, 1.0 / jnp.sum(e_c, axis=-1, keepdims=True), 0.0)
    e_c = e_c.astype(BF16)
    v_ovl = jnp.concatenate([vc_ref[0], ovl_ref[...]], axis=1)
    pv_c = (by_group(lambda rows: _dot(e_c[rows], v_ovl)) * inv_c).reshape(H, QB, LANES + ns)
    o_cmp = pv_c[:, :, :LANES]
    bias = []
    for g in range(NSA_KV_GROUPS):
        imp_t = jnp.sum(pv_c[g * R:(g + 1) * R, :, LANES:], axis=0).T
        picked_t = forced_t | (_top_rows(jnp.where(free_t, imp_t, NEG_INF), j_row,
                                         n_sel - N_FORCED) & valid_t)
        bias_g = jnp.where(picked_t & (j_row < first_own_block), 0.0, NEG_INF).T.astype(BF16)
        bias += [bias_g] * R

    s_wd = masked_scores(
        jnp.concatenate([kw_ref[0, pl.ds(win_start, WIN_KEYS), :], ks_ref[0, pl.ds(start, QB), :]], axis=0),
        jnp.concatenate([wbias_ref[jnp.minimum(c, WINDOW // QB)], tbias_ref[...]], axis=0))
    s_w, s_d = s_wd[:, :WIN_KEYS], s_wd[:, WIN_KEYS:]
    e_w = jnp.exp2(s_w - jnp.max(s_w, axis=-1, keepdims=True))
    e_w = e_w.astype(BF16)
    vw1 = with_ones(vw_ref[0, pl.ds(win_start, WIN_KEYS), :])
    acc_win = by_group(lambda rows: _dot(e_w[rows], vw1))

    lhs_ref[:, :LANES] = q_all
    lhs_ref[:, LANES:] = jnp.concatenate(bias, axis=0)
    m_run = jnp.max(s_d, axis=-1, keepdims=True)
    p_d = jnp.exp2(s_d - m_run).astype(BF16)

    def group_values(v):
        return [jnp.where(low, v, 1.0), jnp.where(low, 1.0, v)]

    v_d = group_values(vs_ref[0, pl.ds(start, QB), :])
    carry = []
    for g, rows in enumerate(group_rows):
        carry += [m_run[rows], _dot(p_d[rows], v_d[g])]

    gates = gates_ref[0].astype(BF16)
    g_cmp, g_sel, g_win = (_dot(gates, gexp_ref[k]) for k in range(3))

    def pair(x, r):
        x = x.reshape(H, QB, x.shape[-1])
        return jnp.where(low, x[r], x[R + r])

    tile_cols = [slice(r * LANES, (r + 1) * LANES) for r in range(R)]
    cmp_win = [g_cmp[:, tile_cols[r]] * pair(o_cmp, r)
               + g_win[:, tile_cols[r]] * (pair(acc_win[:, :LANES], r) * (1.0 / pair(acc_win[:, LANES:], r)))
               for r in range(R)]

    def sel_body(kb, carry):
        k0 = pl.multiple_of(kb * SEL_KEY_CHUNK, SEL_KEY_CHUNK)
        ke = jnp.concatenate([ks_ref[0, pl.ds(k0, SEL_KEY_CHUNK), :],
                              onehot_ref[pl.ds(k0, SEL_KEY_CHUNK), :]], axis=1)
        s = _dot_nt(lhs_ref[...], ke)
        v = group_values(vs_ref[0, pl.ds(k0, SEL_KEY_CHUNK), :])
        out = []
        for g, rows in enumerate(group_rows):
            m_run, acc = carry[2 * g:2 * g + 2]
            m_new = jnp.maximum(m_run, jnp.max(s[rows], axis=-1, keepdims=True))
            p = jnp.exp2(s[rows] - m_new).astype(BF16)
            out += [m_new, jnp.exp2(m_run - m_new) * acc + _dot(p, v[g])]
        return tuple(out)

    def sel_body_pair(kp, carry):
        return sel_body(2 * kp + 1, sel_body(2 * kp, carry))

    n_chunks = (start + SEL_KEY_CHUNK - 1) // SEL_KEY_CHUNK
    carry = lax.fori_loop(0, n_chunks // 2, sel_body_pair, tuple(carry))
    carry = lax.fori_loop(n_chunks - n_chunks % 2, n_chunks, sel_body, carry)

    sel0 = carry[1].reshape(R, QB, LANES)
    sel1 = carry[3].reshape(R, QB, LANES)
    tiles = []
    for r in range(R):
        sums = pltpu.roll(jnp.where(low, sel1[r], sel0[r]), HEAD_DIM, axis=1)
        o_sel = jnp.where(low, sel0[r], sel1[r]) * (1.0 / sums)
        tiles.append(cmp_win[r] + g_sel[:, tile_cols[r]] * o_sel)
    o_ref[0] = jnp.concatenate(tiles, axis=1).astype(o_ref.dtype)


def _nsa(q, kc, vc, ks, vs, kw, vw, gates, consts):
    B, H, S, _ = q.shape
    ncp = kc.shape[1]
    ns = consts[0].shape[1]
    nq = S // Q_BLOCK
    whole = lambda n: pl.BlockSpec((1, n, KV_PAIR), lambda b, c: (b, 0, 0))
    return pl.pallas_call(
        functools.partial(_nsa_kernel, n_sel=min(SEL_TOP_N, ns)),
        grid=(B, nq),
        in_specs=[
            pl.BlockSpec((1, H, Q_BLOCK, LANES), lambda b, c: (b, 0, c, 0)),
            whole(ncp), whole(ncp), whole(S), whole(S), whole(S), whole(S),
            pl.BlockSpec((1, Q_BLOCK, LANES), lambda b, c: (b, c, 0)),
        ] + [pl.BlockSpec(a.shape, lambda b, c, nd=a.ndim: (0,) * nd) for a in consts],
        out_specs=pl.BlockSpec((1, Q_BLOCK, H * HEAD_DIM), lambda b, c: (b, c, 0)),
        out_shape=jax.ShapeDtypeStruct((B, S, H * HEAD_DIM), BF16),
        scratch_shapes=[pltpu.VMEM((H * Q_BLOCK, LANES + ns), BF16)],
        compiler_params=pltpu.CompilerParams(
            dimension_semantics=("parallel", "arbitrary"), vmem_limit_bytes=VMEM_LIMIT),
        name="nsa",
    )(q, kc, vc, ks, vs, kw, vw, gates, *consts)


def _merge_kernel(x_ref, nsa_ref, sgu_ref, gate_ref, wbn_ref, wbs_ref, wout_ref, g_ref, b_ref,
                  o_ref, *, alpha):
    tm, d = x_ref.shape
    sub = tm // MERGE_SUB_BLOCKS
    for i in range(MERGE_SUB_BLOCKS):
        rows = slice(i * sub, (i + 1) * sub)
        a = _dot(nsa_ref[rows, :], wbn_ref[...])
        s = _dot(sgu_ref[rows, :], wbs_ref[...])
        merged = gate_ref[rows, :d].astype(F32) * a + gate_ref[rows, d:].astype(F32) * s
        mix = _dot(merged.astype(BF16), wout_ref[...])
        o_ref[rows, :] = _layer_norm(alpha * x_ref[rows, :] + mix, g_ref[...], b_ref[...])


def _merge(x2, o_nsa, o_sgu, merge_g, wbn, wbs, wout, ln_g, ln_b, alpha):
    T, D = x2.shape
    tm = MERGE_TOKEN_TILE
    row = lambda w: pl.BlockSpec((tm, w), lambda i: (i, 0))
    full = lambda a: pl.BlockSpec(a.shape, lambda i: (0, 0))
    return pl.pallas_call(
        functools.partial(_merge_kernel, alpha=alpha),
        grid=(T // tm,),
        in_specs=[row(D), row(o_nsa.shape[1]), row(o_sgu.shape[1]), row(2 * D),
                  full(wbn), full(wbs), full(wout), full(ln_g), full(ln_b)],
        out_specs=row(D),
        out_shape=jax.ShapeDtypeStruct((T, D), F32),
        compiler_params=pltpu.CompilerParams(
            dimension_semantics=("parallel",), vmem_limit_bytes=VMEM_LIMIT),
        name="merge",
    )(x2, o_nsa, o_sgu, merge_g, wbn, wbs, wout, ln_g, ln_b)


def _ffn_kernel(h_ref, wg_ref, wu_ref, wd_ref, g_ref, b_ref, o_ref, *, alpha):
    sub = h_ref.shape[0] // FFN_SUB_BLOCKS
    for i in range(FFN_SUB_BLOCKS):
        rows = slice(i * sub, (i + 1) * sub)
        h = h_ref[rows, :]
        hb = h.astype(BF16)
        act = jax.nn.silu(_dot(hb, wg_ref[...])) * _dot(hb, wu_ref[...])
        ffn = _dot(act.astype(BF16), wd_ref[...])
        o_ref[rows, :] = _layer_norm(alpha * h + ffn, g_ref[...], b_ref[...])


def _ffn(h, wg, wu, wd, ln_g, ln_b, alpha):
    T, D = h.shape
    tm = FFN_TOKEN_TILE
    resident = lambda a: pl.BlockSpec(a.shape, lambda i: (0, 0), pipeline_mode=pl.Buffered(1))
    return pl.pallas_call(
        functools.partial(_ffn_kernel, alpha=alpha),
        grid=(T // tm,),
        in_specs=[pl.BlockSpec((tm, D), lambda i: (i, 0)), resident(wg), resident(wu), resident(wd),
                  resident(ln_g), resident(ln_b)],
        out_specs=pl.BlockSpec((tm, D), lambda i: (i, 0)),
        out_shape=jax.ShapeDtypeStruct((T, D), F32),
        compiler_params=pltpu.CompilerParams(
            dimension_semantics=("parallel",), vmem_limit_bytes=VMEM_LIMIT),
        name="ffn",
    )(h, wg, wu, wd, ln_g, ln_b)


def _split_sizes(d_model):
    return [Q_WIDTH] + [KV_PAIR] * 6 + [3 * NSA_HEADS, SGU_WIDTH, SGU_WIDTH, 2 * d_model]


def _nsa_constants(S):
    ncp = S // CMP_STRIDE
    ns = S // SEL_BLOCK
    QB = Q_BLOCK
    cs = np.arange(ncp)[:, None] * CMP_STRIDE
    ss = np.arange(ns)[None, :] * SEL_BLOCK
    overlap = np.clip(np.minimum(cs + CMP_BLOCK, ss + SEL_BLOCK) - np.maximum(cs, ss), 0, None) / CMP_BLOCK
    onehot = (np.arange(S)[:, None] // SEL_BLOCK) == np.arange(ns)[None, :]
    gate_expand = np.zeros((3, LANES, Q_WIDTH), np.float32)
    for head in range(NSA_HEADS):
        grp, r = divmod(head, NSA_GROUP_SIZE)
        lo = r * LANES + grp * HEAD_DIM
        for k in range(3):
            gate_expand[k, 3 * head + k, lo:lo + HEAD_DIM] = 1.0
    tok = np.tile(np.eye(QB, dtype=np.float32), (NSA_HEADS, 1))
    tl = np.arange(QB)[None, :]
    bias = lambda visible: np.where(visible, 0.0, NEG_INF).astype(np.float32)
    i_rel = np.arange(2 * ncp)[:, None] - ncp
    cmp_bias = bias(CMP_STRIDE * i_rel + CMP_BLOCK - 1 <= tl)
    kk = np.arange(WIN_KEYS)[:, None]
    n_clip = WINDOW // QB
    win_bias = np.stack([bias(kk <= c * QB + tl) for c in range(n_clip)]
                        + [bias((kk <= WINDOW + tl) & (kk > tl))])
    own_bias = bias(np.arange(QB)[:, None] <= tl)
    as_bf16 = lambda a: jnp.asarray(a, dtype=BF16)
    return (as_bf16(overlap), as_bf16(onehot), as_bf16(gate_expand), as_bf16(tok),
            jnp.asarray(cmp_bias), as_bf16(win_bias), as_bf16(own_bias))


def _compress_weights(w1):
    hid = w1.shape[1]
    w = w1.reshape(2, CMP_STRIDE, 1, HEAD_DIM, hid)
    eye = jnp.eye(NSA_KV_GROUPS, dtype=w1.dtype)
    wg = w[:, None] * eye[None, :, None, :, None, None]
    return wg.reshape(2, NSA_KV_GROUPS, CMP_STRIDE * KV_PAIR, hid).astype(BF16)


def _compress_pe(pe):
    p = jnp.broadcast_to(pe.reshape(2, CMP_STRIDE, 1, HEAD_DIM), (2, CMP_STRIDE, NSA_KV_GROUPS, HEAD_DIM))
    return p.reshape(2, CMP_STRIDE * KV_PAIR)


def kernel(x, positions, w_in, pe_ck, w_ck1, w_ck2, pe_cv, w_cv1, w_cv2, ln_sgu_g, ln_sgu_b,
           w_spatial, b_spatial, w_branch_nsa, w_branch_sgu, w_out, ln1_g, ln1_b,
           w_ffn_gate, w_ffn_up, w_ffn_down, ln2_g, ln2_b):
    B, S, D = x.shape
    depth = w_in.shape[0]
    alpha = (2.0 * depth) ** 0.25
    assert S % TOKEN_TILE == 0 and S % SEL_KEY_CHUNK == 0 and S >= WIN_KEYS
    ncp = S // CMP_STRIDE
    ns = S // SEL_BLOCK

    freqs = ROPE_THETA ** (-jnp.arange(ROPE_HALF, dtype=F32) / ROPE_HALF)
    freqs = freqs.reshape(ROPE_HALF, 1)
    pos_rows = positions.astype(F32).reshape(B * S // TOKEN_TILE, 1, TOKEN_TILE)

    nsa_consts = _nsa_constants(S)

    sizes = _split_sizes(D)
    offs = np.concatenate([[0], np.cumsum(sizes)])
    seg = lambda w, i: w[:, offs[i]:offs[i + 1]]

    h = x.reshape(B * S, D)
    for l in range(depth):
        w = w_in[l]
        gate_cols = jnp.pad(seg(w, 7), ((0, 0), (0, LANES - sizes[7])))
        w_all = jnp.concatenate([seg(w, i) for i in (0, 1, 2, 3, 4, 5, 6, 8, 9, 10)] + [gate_cols],
                                axis=1).astype(BF16)
        bsp = jnp.repeat(b_spatial[l].T, SGU_WIDTH // SGU_GROUPS, axis=1)
        q, k_c, v_c, k_s, v_s, k_w, v_w, gates, o_sgu, merge_g = _inproj(
            h.reshape(B, S, D), pos_rows, freqs, w_all,
            ln_sgu_g[l].reshape(1, -1), ln_sgu_b[l].reshape(1, -1), w_spatial[l], bsp)

        kc, vc = _compress(
            k_c, v_c,
            _compress_pe(pe_ck[l]), _compress_weights(w_ck1[l]), w_ck2[l].astype(BF16),
            _compress_pe(pe_cv[l]), _compress_weights(w_cv1[l]), w_cv2[l].astype(BF16))

        o_nsa = _nsa(q, kc, vc, k_s, v_s, k_w, v_w, gates, nsa_consts)

        wbn = w_branch_nsa[l].reshape(NSA_KV_GROUPS, NSA_GROUP_SIZE, HEAD_DIM, -1)
        wbn = wbn.transpose(1, 0, 2, 3).reshape(Q_WIDTH, -1)
        h = _merge(h, o_nsa.reshape(B * S, -1), o_sgu, merge_g,
                   wbn.astype(BF16), w_branch_sgu[l].astype(BF16), w_out[l].astype(BF16),
                   ln1_g[l].reshape(1, -1), ln1_b[l].reshape(1, -1), alpha)
        h = _ffn(h, w_ffn_gate[l].astype(BF16), w_ffn_up[l].astype(BF16), w_ffn_down[l].astype(BF16),
                 ln2_g[l].reshape(1, -1), ln2_b[l].reshape(1, -1), alpha)
    return h.reshape(B, S, D)
```

```python
import functools

import numpy as np
import jax
import jax.numpy as jnp
from jax import lax
from jax.experimental import pallas as pl
from jax.experimental.pallas import tpu as pltpu

F32 = jnp.float32
BF16 = jnp.bfloat16

HEAD_DIM = 64
NSA_HEADS = 8
NSA_KV_GROUPS = 2
NSA_GROUP_SIZE = NSA_HEADS // NSA_KV_GROUPS
CMP_BLOCK = 32
CMP_STRIDE = 16
SEL_BLOCK = 64
SEL_TOP_N = 16
WINDOW = 512
Q_BLOCK = 128
FORCE_BONUS = 1.0e4
N_FORCED = 3
assert FORCE_BONUS > NSA_GROUP_SIZE and SEL_TOP_N >= N_FORCED
ROPE_THETA = 500000.0
ROPE_DIM = HEAD_DIM // 4
ROPE_HALF = ROPE_DIM // 2
SGU_GROUPS = 4
SGU_CHUNK = 128
SGU_WIDTH = 512
LN_EPS = 1e-5
NEG_INF = -1e30

LANES = 128
KV_PAIR = NSA_KV_GROUPS * HEAD_DIM
assert KV_PAIR == LANES

TOKEN_TILE = 1024
INPROJ_SUB_BLOCKS = 4
MERGE_TOKEN_TILE = 1024
MERGE_SUB_BLOCKS = 4
FFN_TOKEN_TILE = 1024
FFN_SUB_BLOCKS = 4
SEL_KEY_CHUNK = 1024
WIN_KEYS = WINDOW + Q_BLOCK
VMEM_LIMIT = 56 * 1024 * 1024


def _layer_norm(x, g, b):
    mu = jnp.mean(x, axis=-1, keepdims=True)
    xc = x - mu
    var = jnp.mean(xc * xc, axis=-1, keepdims=True)
    return xc * lax.rsqrt(var + LN_EPS) * g + b


def _dot(a, b):
    return jnp.dot(a, b, preferred_element_type=F32)


def _dot_nt(a, b):
    return lax.dot_general(a, b, (((1,), (1,)), ((), ())), preferred_element_type=F32)


Q_WIDTH = NSA_HEADS * HEAD_DIM
COL_Q = 0
COL_KV = COL_Q + Q_WIDTH
COL_U = COL_KV + 6 * KV_PAIR
COL_V = COL_U + SGU_WIDTH
COL_MERGE = COL_V + SGU_WIDTH
ROPED_KV = (0, 2, 4)
COMPRESSED_KV = (0, 1)
LOG2_E = 1.4426950408889634


def _inproj_kernel(x_ref, pos_ref, freq_ref, w_ref, lng_ref, lnb_ref, wsp_ref, bsp_ref,
                   q_ref, kc_ref, vc_ref, ks_ref, vs_ref, kw_ref, vw_ref,
                   gates_ref, sgu_ref, merge_ref, stage_ref, *, d_model):
    col_gates = COL_MERGE + 2 * d_model
    lane = lax.broadcasted_iota(jnp.int32, (1, LANES), 1)
    rest = HEAD_DIM - ROPE_DIM
    scale = HEAD_DIM ** -0.5 * LOG2_E
    kv_refs = (kc_ref, vc_ref, ks_ref, vs_ref, kw_ref, vw_ref)
    row = lax.broadcasted_iota(jnp.int32, (SGU_CHUNK, SGU_CHUNK), 0)
    col = lax.broadcasted_iota(jnp.int32, (SGU_CHUNK, SGU_CHUNK), 1)
    gdim = SGU_WIDTH // SGU_GROUPS
    w_sp = [jnp.where(col <= row, wsp_ref[g], 0.0).astype(BF16) for g in range(SGU_GROUPS)]

    sub = x_ref.shape[0] // INPROJ_SUB_BLOCKS
    for blk in range(INPROJ_SUB_BLOCKS):
        rows = slice(blk * sub, (blk + 1) * sub)
        xb = x_ref[rows, :].astype(BF16)

        u = jax.nn.gelu(_dot(xb, w_ref[:, COL_U:COL_U + SGU_WIDTH]))
        v = jax.nn.gelu(_dot(xb, w_ref[:, COL_V:COL_V + SGU_WIDTH]))
        v = _layer_norm(v, lng_ref[...], lnb_ref[...]).astype(BF16)

        rm = _dot(xb, w_ref[:, COL_MERGE:COL_MERGE + 2 * d_model])
        merge_ref[rows, :] = jax.nn.sigmoid(rm).astype(merge_ref.dtype)
        rg = _dot(xb, w_ref[:, col_gates:col_gates + LANES])
        gates_ref[0, rows, :] = jax.nn.sigmoid(rg)

        ang = freq_ref[...] * pos_ref[0, :, rows]
        cos_f = jnp.cos(ang)
        sin_f = jnp.sin(ang)
        per_head = lambda parts: jnp.concatenate(parts * (LANES // HEAD_DIM), axis=0).T
        cos_t = per_head([cos_f, cos_f, jnp.ones((rest, sub), F32)])
        sin_lo = per_head([-sin_f, jnp.zeros((ROPE_HALF + rest, sub), F32)])
        sin_hi = per_head([jnp.zeros((ROPE_HALF, sub), F32), sin_f, jnp.zeros((rest, sub), F32)])

        def rope(t):
            return (t * cos_t + pltpu.roll(t, LANES - ROPE_HALF, axis=1) * sin_lo
                    + pltpu.roll(t, ROPE_HALF, axis=1) * sin_hi)

        rq = _dot(xb, w_ref[:, COL_Q:COL_Q + Q_WIDTH])
        for pair in range(NSA_HEADS // 2):
            t = rope(rq[:, pair * LANES:(pair + 1) * LANES]) * scale
            t_sw = pltpu.roll(t, HEAD_DIM, axis=1)
            for half in range(2):
                h = 2 * pair + half
                grp = h // NSA_GROUP_SIZE
                src = t if half == grp else t_sw
                keep = (lane >= HEAD_DIM) if grp == 1 else (lane < HEAD_DIM)
                q_ref[0, h, rows, :] = jnp.where(keep, src, 0.0).astype(q_ref.dtype)

        rkv = _dot(xb, w_ref[:, COL_KV:COL_KV + 6 * KV_PAIR])
        for i, ref in enumerate(kv_refs):
            t = rkv[:, i * KV_PAIR:(i + 1) * KV_PAIR]
            if i in ROPED_KV:
                t = rope(t)
            if i in COMPRESSED_KV:
                stage_ref[i] = t
                n_rows = sub // CMP_STRIDE
                slabs = [stage_ref[i, pl.ds(tok, n_rows, stride=CMP_STRIDE), :] for tok in range(CMP_STRIDE)]
                ref[0, blk * n_rows:(blk + 1) * n_rows, :] = jnp.concatenate(slabs, axis=1).astype(ref.dtype)
            else:
                ref[0, rows, :] = t.astype(ref.dtype)

        for n in range(sub // SGU_CHUNK):
            chunk = slice(n * SGU_CHUNK, (n + 1) * SGU_CHUNK)
            mixed = jnp.concatenate(
                [_dot(w_sp[g], v[chunk, g * gdim:(g + 1) * gdim]) for g in range(SGU_GROUPS)], axis=1)
            out_rows = slice(blk * sub + n * SGU_CHUNK, blk * sub + (n + 1) * SGU_CHUNK)
            sgu_ref[out_rows, :] = (u[chunk, :] * (mixed + bsp_ref[...])).astype(sgu_ref.dtype)


def _inproj(x, pos_rows, freqs, w_all, ln_g, ln_b, w_sp, b_sp):
    B, S, D = x.shape
    tm = TOKEN_TILE
    nt = S // tm
    wcols = w_all.shape[1]
    x2 = x.reshape(B * S, D)
    kv_shape = jax.ShapeDtypeStruct((B, S, KV_PAIR), BF16)
    kv_spec = pl.BlockSpec((1, tm, KV_PAIR), lambda b, i: (b, i, 0))
    cmp_shape = jax.ShapeDtypeStruct((B, S // CMP_STRIDE, CMP_STRIDE * KV_PAIR), BF16)
    cmp_spec = pl.BlockSpec((1, tm // CMP_STRIDE, CMP_STRIDE * KV_PAIR), lambda b, i: (b, i, 0))
    const2 = lambda b, i: (0, 0)
    return pl.pallas_call(
        functools.partial(_inproj_kernel, d_model=D),
        grid=(B, nt),
        in_specs=[
            pl.BlockSpec((tm, D), lambda b, i: (b * nt + i, 0)),
            pl.BlockSpec((1, 1, tm), lambda b, i: (b * nt + i, 0, 0)),
            pl.BlockSpec((ROPE_HALF, 1), const2),
            pl.BlockSpec((D, wcols), const2, pipeline_mode=pl.Buffered(1)),
            pl.BlockSpec((1, SGU_WIDTH), const2),
            pl.BlockSpec((1, SGU_WIDTH), const2),
            pl.BlockSpec((SGU_GROUPS, SGU_CHUNK, SGU_CHUNK), lambda b, i: (0, 0, 0)),
            pl.BlockSpec((SGU_CHUNK, SGU_WIDTH), const2),
        ],
        out_specs=[
            pl.BlockSpec((1, NSA_HEADS, tm, LANES), lambda b, i: (b, 0, i, 0)),
            cmp_spec, cmp_spec, kv_spec, kv_spec, kv_spec, kv_spec,
            pl.BlockSpec((1, tm, LANES), lambda b, i: (b, i, 0)),
            pl.BlockSpec((tm, SGU_WIDTH), lambda b, i: (b * nt + i, 0)),
            pl.BlockSpec((tm, 2 * D), lambda b, i: (b * nt + i, 0)),
        ],
        out_shape=[
            jax.ShapeDtypeStruct((B, NSA_HEADS, S, LANES), BF16),
            cmp_shape, cmp_shape, kv_shape, kv_shape, kv_shape, kv_shape,
            jax.ShapeDtypeStruct((B, S, LANES), F32),
            jax.ShapeDtypeStruct((B * S, SGU_WIDTH), BF16),
            jax.ShapeDtypeStruct((B * S, 2 * D), BF16),
        ],
        scratch_shapes=[pltpu.VMEM((len(COMPRESSED_KV), tm // INPROJ_SUB_BLOCKS, KV_PAIR), F32)],
        compiler_params=pltpu.CompilerParams(
            dimension_semantics=("parallel", "parallel"), vmem_limit_bytes=VMEM_LIMIT),
        name="inproj",
    )(x2, pos_rows, freqs, w_all, ln_g, ln_b, w_sp, b_sp)


def _compress_kernel(k_ref, v_ref, kpe_ref, kw1_ref, kw2_ref, vpe_ref, vw1_ref, vw2_ref,
                     ko_ref, vo_ref):
    ncp = k_ref.shape[1]

    def one(tok_ref, pe_ref, w1_ref, w2_ref, out_ref):
        a = tok_ref[0]
        width = pe_ref.shape[1]
        pe_first = jnp.broadcast_to(pe_ref[0:1, :], (8, width)).astype(BF16)
        pe_second = jnp.broadcast_to(pe_ref[1:2, :], (8, width)).astype(BF16)
        outs = []
        for g in range(NSA_KV_GROUPS):
            first = _dot(a, w1_ref[0, g])
            second = _dot(a, w1_ref[1, g])
            second = pltpu.roll(second, ncp - 1, axis=0)
            bias = (_dot(pe_first, w1_ref[0, g]) + _dot(pe_second, w1_ref[1, g]))[0:1]
            hid = jax.nn.silu(first + second + bias).astype(BF16)
            outs.append(_dot(hid, w2_ref[...]))
        out_ref[0] = jnp.concatenate(outs, axis=1).astype(out_ref.dtype)

    one(k_ref, kpe_ref, kw1_ref, kw2_ref, ko_ref)
    one(v_ref, vpe_ref, vw1_ref, vw2_ref, vo_ref)


def _compress(k_tok, v_tok, kpe, kw1, kw2, vpe, vw1, vw2):
    B, ncp, width = k_tok.shape
    tok_spec = pl.BlockSpec((1, ncp, width), lambda b: (b, 0, 0))
    pe_spec = pl.BlockSpec(kpe.shape, lambda b: (0, 0))
    w1_spec = pl.BlockSpec(kw1.shape, lambda b: (0, 0, 0, 0))
    w2_spec = pl.BlockSpec(kw2.shape, lambda b: (0, 0))
    out_spec = pl.BlockSpec((1, ncp, KV_PAIR), lambda b: (b, 0, 0))
    out_shape = jax.ShapeDtypeStruct((B, ncp, KV_PAIR), BF16)
    return pl.pallas_call(
        _compress_kernel,
        grid=(B,),
        in_specs=[tok_spec, tok_spec, pe_spec, w1_spec, w2_spec, pe_spec, w1_spec, w2_spec],
        out_specs=[out_spec, out_spec],
        out_shape=[out_shape, out_shape],
        compiler_params=pltpu.CompilerParams(
            dimension_semantics=("parallel",), vmem_limit_bytes=VMEM_LIMIT),
        name="compress",
    )(k_tok, v_tok, kpe, kw1, kw2, vpe, vw1, vw2)


KNOCKED_OUT = -(2.0 ** 127)


def _top_rows(s, row_idx, rounds):
    n = float(s.shape[0])
    row_idx = row_idx.astype(F32)
    for _ in range(rounds):
        m = jnp.max(s, axis=0, keepdims=True)
        first = jnp.min(jnp.where(s == m, row_idx, n), axis=0, keepdims=True)
        s = jnp.where(row_idx == first, KNOCKED_OUT, s)
    return s <= KNOCKED_OUT


def _nsa_kernel(q_ref, kc_ref, vc_ref, ks_ref, vs_ref, kw_ref, vw_ref, gates_ref,
                ovl_ref, onehot_ref, gexp_ref, tok_ref, cbias_ref, wbias_ref, tbias_ref,
                o_ref, lhs_ref, *, n_sel):
    R = NSA_GROUP_SIZE
    QB = Q_BLOCK
    ncp = kc_ref.shape[1]
    ns = ovl_ref.shape[1]
    c = pl.program_id(1)
    start = c * QB
    t_col = start + lax.broadcasted_iota(jnp.int32, (QB, 1), 0)
    lane = lax.broadcasted_iota(jnp.int32, (1, LANES), 1)

    row_valid = jnp.concatenate([t_col >= CMP_BLOCK - 1] * NSA_HEADS, axis=0)

    cur_row = (start + lax.broadcasted_iota(jnp.int32, (1, QB), 1)) // SEL_BLOCK
    j_row = lax.broadcasted_iota(jnp.int32, (ns, QB), 0)
    valid_t = j_row <= cur_row
    forced_t = (j_row == 0) | (j_row == cur_row) | (j_row == cur_row - 1)
    free_t = valid_t & jnp.logical_not(forced_t)

    win_start = pl.multiple_of(jnp.maximum(start - WINDOW, 0), QB)
    first_own_block = start // SEL_BLOCK

    H = NSA_HEADS
    q_all = q_ref[0].reshape(H * QB, LANES)
    ones = jnp.ones((WIN_KEYS, LANES), BF16)
    low = lane < HEAD_DIM

    def with_ones(v):
        return jnp.concatenate([v, ones[:v.shape[0]]], axis=1)

    def group_values(v):
        return [jnp.where(low, v, 1.0), jnp.where(low, 1.0, v)]

    group_rows = [slice(g * R * QB, (g + 1) * R * QB) for g in range(NSA_KV_GROUPS)]

    def by_group(f):
        return jnp.concatenate([f(rows) for rows in group_rows], axis=0)

    q_tok = jnp.concatenate([q_all, tok_ref[...]], axis=1)

    def masked_scores(keys, key_bias):
        k_aug = jnp.concatenate([keys, key_bias], axis=1)
        return by_group(lambda rows: _dot_nt(q_tok[rows], k_aug))

    cmp_rows = pl.multiple_of(ncp - c * (QB // CMP_STRIDE), 8)
    s_c = masked_scores(kc_ref[0], cbias_ref[pl.ds(cmp_rows, ncp), :].astype(BF16))
    e_c = jnp.exp2(s_c - jnp.max(s_c, axis=-1, keepdims=True))
    inv_c = jnp.where(row_valid, 1.0 / jnp.sum(e_c, axis=-1, keepdims=True), 0.0)
    e_c = e_c.astype(BF16)
    v_ovl = jnp.concatenate([vc_ref[0], ovl_ref[...]], axis=1)
    pv_c = (by_group(lambda rows: _dot(e_c[rows], v_ovl)) * inv_c).reshape(H, QB, LANES + ns)
    o_cmp = pv_c[:, :, :LANES]
    both = lambda a: jnp.concatenate([a] * NSA_KV_GROUPS, axis=1)
    imp_t = jnp.concatenate([jnp.sum(pv_c[g * R:(g + 1) * R, :, LANES:], axis=0).T
                             for g in range(NSA_KV_GROUPS)], axis=1)
    picked_t = both(forced_t) | (_top_rows(jnp.where(both(free_t), imp_t, NEG_INF), both(j_row),
                                           n_sel - N_FORCED) & both(valid_t))
    bias_t = jnp.where(picked_t & (both(j_row) < first_own_block), 0.0, NEG_INF)
    bias = []
    for g in range(NSA_KV_GROUPS):
        bias += [bias_t[:, g * QB:(g + 1) * QB].T.astype(BF16)] * R

    s_wd = masked_scores(
        jnp.concatenate([kw_ref[0, pl.ds(win_start, WIN_KEYS), :], ks_ref[0, pl.ds(start, QB), :]], axis=0),
        jnp.concatenate([wbias_ref[jnp.minimum(c, WINDOW // QB)], tbias_ref[...]], axis=0))
    s_w, s_d = s_wd[:, :WIN_KEYS], s_wd[:, WIN_KEYS:]
    e_w = jnp.exp2(s_w - jnp.max(s_w, axis=-1, keepdims=True))
    e_w = e_w.astype(BF16)
    vw1 = with_ones(vw_ref[0, pl.ds(win_start, WIN_KEYS), :])
    acc_win = by_group(lambda rows: _dot(e_w[rows], vw1))

    lhs_ref[:, :LANES] = q_all
    lhs_ref[:, LANES:] = jnp.concatenate(bias, axis=0)
    m_run = jnp.max(s_d, axis=-1, keepdims=True)
    p_d = jnp.exp2(s_d - m_run).astype(BF16)

    v_d = group_values(vs_ref[0, pl.ds(start, QB), :])
    carry = []
    for g, rows in enumerate(group_rows):
        carry += [m_run[rows], _dot(p_d[rows], v_d[g])]

    gates = gates_ref[0].astype(BF16)
    g_cmp, g_sel, g_win = (_dot(gates, gexp_ref[k]) for k in range(3))

    def pair(x, r):
        x = x.reshape(H, QB, x.shape[-1])
        return jnp.where(low, x[r], x[R + r])

    tile_cols = [slice(r * LANES, (r + 1) * LANES) for r in range(R)]
    cmp_win = [g_cmp[:, tile_cols[r]] * pair(o_cmp, r)
               + g_win[:, tile_cols[r]] * (pair(acc_win[:, :LANES], r) * (1.0 / pair(acc_win[:, LANES:], r)))
               for r in range(R)]

    def sel_body(kb, carry):
        k0 = pl.multiple_of(kb * SEL_KEY_CHUNK, SEL_KEY_CHUNK)
        ke = jnp.concatenate([ks_ref[0, pl.ds(k0, SEL_KEY_CHUNK), :],
                              onehot_ref[pl.ds(k0, SEL_KEY_CHUNK), :]], axis=1)
        s = _dot_nt(lhs_ref[...], ke)
        v = group_values(vs_ref[0, pl.ds(k0, SEL_KEY_CHUNK), :])
        out = []
        for g, rows in enumerate(group_rows):
            m_run, acc = carry[2 * g:2 * g + 2]
            m_new = jnp.maximum(m_run, jnp.max(s[rows], axis=-1, keepdims=True))
            p = jnp.exp2(s[rows] - m_new).astype(BF16)
            out += [m_new, jnp.exp2(m_run - m_new) * acc + _dot(p, v[g])]
        return tuple(out)

    def sel_body_pair(kp, carry):
        return sel_body(2 * kp + 1, sel_body(2 * kp, carry))

    n_chunks = (start + SEL_KEY_CHUNK - 1) // SEL_KEY_CHUNK
    carry = lax.fori_loop(0, n_chunks // 2, sel_body_pair, tuple(carry))
    carry = lax.fori_loop(n_chunks - n_chunks % 2, n_chunks, sel_body, carry)

    sel0 = carry[1].reshape(R, QB, LANES)
    sel1 = carry[3].reshape(R, QB, LANES)
    tiles = []
    for r in range(R):
        sums = pltpu.roll(jnp.where(low, sel1[r], sel0[r]), HEAD_DIM, axis=1)
        o_sel = jnp.where(low, sel0[r], sel1[r]) * (1.0 / sums)
        tiles.append(cmp_win[r] + g_sel[:, tile_cols[r]] * o_sel)
    o_ref[0] = jnp.concatenate(tiles, axis=1).astype(o_ref.dtype)


def _nsa(q, kc, vc, ks, vs, kw, vw, gates, consts):
    B, H, S, _ = q.shape
    ncp = kc.shape[1]
    ns = consts[0].shape[1]
    nq = S // Q_BLOCK
    whole = lambda n: pl.BlockSpec((1, n, KV_PAIR), lambda b, c: (b, 0, 0))
    return pl.pallas_call(
        functools.partial(_nsa_kernel, n_sel=min(SEL_TOP_N, ns)),
        grid=(B, nq),
        in_specs=[
            pl.BlockSpec((1, H, Q_BLOCK, LANES), lambda b, c: (b, 0, c, 0)),
            whole(ncp), whole(ncp), whole(S), whole(S), whole(S), whole(S),
            pl.BlockSpec((1, Q_BLOCK, LANES), lambda b, c: (b, c, 0)),
        ] + [pl.BlockSpec(a.shape, lambda b, c, nd=a.ndim: (0,) * nd) for a in consts],
        out_specs=pl.BlockSpec((1, Q_BLOCK, H * HEAD_DIM), lambda b, c: (b, c, 0)),
        out_shape=jax.ShapeDtypeStruct((B, S, H * HEAD_DIM), BF16),
        scratch_shapes=[pltpu.VMEM((H * Q_BLOCK, LANES + ns), BF16)],
        compiler_params=pltpu.CompilerParams(
            dimension_semantics=("parallel", "arbitrary"), vmem_limit_bytes=VMEM_LIMIT),
        name="nsa",
    )(q, kc, vc, ks, vs, kw, vw, gates, *consts)


def _merge_kernel(x_ref, nsa_ref, sgu_ref, gate_ref, wbn_ref, wbs_ref, wout_ref, g_ref, b_ref,
                  o_ref, *, alpha):
    tm, d = x_ref.shape
    sub = tm // MERGE_SUB_BLOCKS
    for i in range(MERGE_SUB_BLOCKS):
        rows = slice(i * sub, (i + 1) * sub)
        a = _dot(nsa_ref[rows, :], wbn_ref[...])
        s = _dot(sgu_ref[rows, :], wbs_ref[...])
        merged = gate_ref[rows, :d].astype(F32) * a + gate_ref[rows, d:].astype(F32) * s
        mix = _dot(merged.astype(BF16), wout_ref[...])
        o_ref[rows, :] = _layer_norm(alpha * x_ref[rows, :] + mix, g_ref[...], b_ref[...])


def _merge(x2, o_nsa, o_sgu, merge_g, wbn, wbs, wout, ln_g, ln_b, alpha):
    T, D = x2.shape
    tm = MERGE_TOKEN_TILE
    row = lambda w: pl.BlockSpec((tm, w), lambda i: (i, 0))
    full = lambda a: pl.BlockSpec(a.shape, lambda i: (0, 0))
    return pl.pallas_call(
        functools.partial(_merge_kernel, alpha=alpha),
        grid=(T // tm,),
        in_specs=[row(D), row(o_nsa.shape[1]), row(o_sgu.shape[1]), row(2 * D),
                  full(wbn), full(wbs), full(wout), full(ln_g), full(ln_b)],
        out_specs=row(D),
        out_shape=jax.ShapeDtypeStruct((T, D), F32),
        compiler_params=pltpu.CompilerParams(
            dimension_semantics=("parallel",), vmem_limit_bytes=VMEM_LIMIT),
        name="merge",
    )(x2, o_nsa, o_sgu, merge_g, wbn, wbs, wout, ln_g, ln_b)


def _ffn_kernel(h_ref, wg_ref, wu_ref, wd_ref, g_ref, b_ref, o_ref, *, alpha):
    sub = h_ref.shape[0] // FFN_SUB_BLOCKS
    for i in range(FFN_SUB_BLOCKS):
        rows = slice(i * sub, (i + 1) * sub)
        h = h_ref[rows, :]
        hb = h.astype(BF16)
        act = jax.nn.silu(_dot(hb, wg_ref[...])) * _dot(hb, wu_ref[...])
        ffn = _dot(act.astype(BF16), wd_ref[...])
        o_ref[rows, :] = _layer_norm(alpha * h + ffn, g_ref[...], b_ref[...])


def _ffn(h, wg, wu, wd, ln_g, ln_b, alpha):
    T, D = h.shape
    tm = FFN_TOKEN_TILE
    resident = lambda a: pl.BlockSpec(a.shape, lambda i: (0, 0), pipeline_mode=pl.Buffered(1))
    return pl.pallas_call(
        functools.partial(_ffn_kernel, alpha=alpha),
        grid=(T // tm,),
        in_specs=[pl.BlockSpec((tm, D), lambda i: (i, 0)), resident(wg), resident(wu), resident(wd),
                  resident(ln_g), resident(ln_b)],
        out_specs=pl.BlockSpec((tm, D), lambda i: (i, 0)),
        out_shape=jax.ShapeDtypeStruct((T, D), F32),
        compiler_params=pltpu.CompilerParams(
            dimension_semantics=("parallel",), vmem_limit_bytes=VMEM_LIMIT),
        name="ffn",
    )(h, wg, wu, wd, ln_g, ln_b)


def _split_sizes(d_model):
    return [Q_WIDTH] + [KV_PAIR] * 6 + [3 * NSA_HEADS, SGU_WIDTH, SGU_WIDTH, 2 * d_model]


def _nsa_constants(S):
    ncp = S // CMP_STRIDE
    ns = S // SEL_BLOCK
    QB = Q_BLOCK
    cs = np.arange(ncp)[:, None] * CMP_STRIDE
    ss = np.arange(ns)[None, :] * SEL_BLOCK
    overlap = np.clip(np.minimum(cs + CMP_BLOCK, ss + SEL_BLOCK) - np.maximum(cs, ss), 0, None) / CMP_BLOCK
    onehot = (np.arange(S)[:, None] // SEL_BLOCK) == np.arange(ns)[None, :]
    gate_expand = np.zeros((3, LANES, Q_WIDTH), np.float32)
    for head in range(NSA_HEADS):
        grp, r = divmod(head, NSA_GROUP_SIZE)
        lo = r * LANES + grp * HEAD_DIM
        for k in range(3):
            gate_expand[k, 3 * head + k, lo:lo + HEAD_DIM] = 1.0
    tok = np.tile(np.eye(QB, dtype=np.float32), (NSA_HEADS, 1))
    tl = np.arange(QB)[None, :]
    bias = lambda visible: np.where(visible, 0.0, NEG_INF).astype(np.float32)
    i_rel = np.arange(2 * ncp)[:, None] - ncp
    cmp_bias = bias(CMP_STRIDE * i_rel + CMP_BLOCK - 1 <= tl)
    kk = np.arange(WIN_KEYS)[:, None]
    n_clip = WINDOW // QB
    win_bias = np.stack([bias(kk <= c * QB + tl) for c in range(n_clip)]
                        + [bias((kk <= WINDOW + tl) & (kk > tl))])
    own_bias = bias(np.arange(QB)[:, None] <= tl)
    as_bf16 = lambda a: jnp.asarray(a, dtype=BF16)
    return (as_bf16(overlap), as_bf16(onehot), as_bf16(gate_expand), as_bf16(tok),
            jnp.asarray(cmp_bias), as_bf16(win_bias), as_bf16(own_bias))


def _compress_weights(w1):
    hid = w1.shape[1]
    w = w1.reshape(2, CMP_STRIDE, 1, HEAD_DIM, hid)
    eye = jnp.eye(NSA_KV_GROUPS, dtype=w1.dtype)
    wg = w[:, None] * eye[None, :, None, :, None, None]
    return wg.reshape(2, NSA_KV_GROUPS, CMP_STRIDE * KV_PAIR, hid).astype(BF16)


def _compress_pe(pe):
    p = jnp.broadcast_to(pe.reshape(2, CMP_STRIDE, 1, HEAD_DIM), (2, CMP_STRIDE, NSA_KV_GROUPS, HEAD_DIM))
    return p.reshape(2, CMP_STRIDE * KV_PAIR)


def kernel(x, positions, w_in, pe_ck, w_ck1, w_ck2, pe_cv, w_cv1, w_cv2, ln_sgu_g, ln_sgu_b,
           w_spatial, b_spatial, w_branch_nsa, w_branch_sgu, w_out, ln1_g, ln1_b,
           w_ffn_gate, w_ffn_up, w_ffn_down, ln2_g, ln2_b):
    B, S, D = x.shape
    depth = w_in.shape[0]
    alpha = (2.0 * depth) ** 0.25
    assert S % TOKEN_TILE == 0 and S % SEL_KEY_CHUNK == 0 and S >= WIN_KEYS
    ncp = S // CMP_STRIDE
    ns = S // SEL_BLOCK

    freqs = ROPE_THETA ** (-jnp.arange(ROPE_HALF, dtype=F32) / ROPE_HALF)
    freqs = freqs.reshape(ROPE_HALF, 1)
    pos_rows = positions.astype(F32).reshape(B * S // TOKEN_TILE, 1, TOKEN_TILE)

    nsa_consts = _nsa_constants(S)

    sizes = _split_sizes(D)
    offs = np.concatenate([[0], np.cumsum(sizes)])
    seg = lambda w, i: w[:, offs[i]:offs[i + 1]]

    h = x.reshape(B * S, D)
    for l in range(depth):
        w = w_in[l]
        gate_cols = jnp.pad(seg(w, 7), ((0, 0), (0, LANES - sizes[7])))
        w_all = jnp.concatenate([seg(w, i) for i in (0, 1, 2, 3, 4, 5, 6, 8, 9, 10)] + [gate_cols],
                                axis=1).astype(BF16)
        bsp = jnp.repeat(b_spatial[l].T, SGU_WIDTH // SGU_GROUPS, axis=1)
        q, k_c, v_c, k_s, v_s, k_w, v_w, gates, o_sgu, merge_g = _inproj(
            h.reshape(B, S, D), pos_rows, freqs, w_all,
            ln_sgu_g[l].reshape(1, -1), ln_sgu_b[l].reshape(1, -1), w_spatial[l], bsp)

        kc, vc = _compress(
            k_c, v_c,
            _compress_pe(pe_ck[l]), _compress_weights(w_ck1[l]), w_ck2[l].astype(BF16),
            _compress_pe(pe_cv[l]), _compress_weights(w_cv1[l]), w_cv2[l].astype(BF16))

        o_nsa = _nsa(q, kc, vc, k_s, v_s, k_w, v_w, gates, nsa_consts)

        wbn = w_branch_nsa[l].reshape(NSA_KV_GROUPS, NSA_GROUP_SIZE, HEAD_DIM, -1)
        wbn = wbn.transpose(1, 0, 2, 3).reshape(Q_WIDTH, -1)
        h = _merge(h, o_nsa.reshape(B * S, -1), o_sgu, merge_g,
                   wbn.astype(BF16), w_branch_sgu[l].astype(BF16), w_out[l].astype(BF16),
                   ln1_g[l].reshape(1, -1), ln1_b[l].reshape(1, -1), alpha)
        h = _ffn(h, w_ffn_gate[l].astype(BF16), w_ffn_up[l].astype(BF16), w_ffn_down[l].astype(BF16),
                 ln2_g[l].reshape(1, -1), ln2_b[l].reshape(1, -1), alpha)
    return h.reshape(B, S, D)
```

```python
import functools

import numpy as np
import jax
import jax.numpy as jnp
from jax import lax
from jax.experimental import pallas as pl
from jax.experimental.pallas import tpu as pltpu

F32 = jnp.float32
BF16 = jnp.bfloat16

HEAD_DIM = 64
NSA_HEADS = 8
NSA_KV_GROUPS = 2
NSA_GROUP_SIZE = NSA_HEADS // NSA_KV_GROUPS
CMP_BLOCK = 32
CMP_STRIDE = 16
SEL_BLOCK = 64
SEL_TOP_N = 16
WINDOW = 512
Q_BLOCK = 128
FORCE_BONUS = 1.0e4
N_FORCED = 3
assert FORCE_BONUS > NSA_GROUP_SIZE and SEL_TOP_N >= N_FORCED
ROPE_THETA = 500000.0
ROPE_DIM = HEAD_DIM // 4
ROPE_HALF = ROPE_DIM // 2
SGU_GROUPS = 4
SGU_CHUNK = 128
SGU_WIDTH = 512
LN_EPS = 1e-5
NEG_INF = -1e30

LANES = 128
KV_PAIR = NSA_KV_GROUPS * HEAD_DIM
assert KV_PAIR == LANES

TOKEN_TILE = 1024
INPROJ_SUB_BLOCKS = 4
MERGE_TOKEN_TILE = 1024
MERGE_SUB_BLOCKS = 4
FFN_TOKEN_TILE = 1024
FFN_SUB_BLOCKS = 4
FUSED_TOKEN_TILE = 512
FUSED_SUB_BLOCKS = 2
SEL_KEY_CHUNK = 1024
WIN_KEYS = WINDOW + Q_BLOCK
VMEM_LIMIT = 56 * 1024 * 1024


def _layer_norm(x, g, b):
    mu = jnp.mean(x, axis=-1, keepdims=True)
    xc = x - mu
    var = jnp.mean(xc * xc, axis=-1, keepdims=True)
    return xc * lax.rsqrt(var + LN_EPS) * g + b


def _dot(a, b):
    return jnp.dot(a, b, preferred_element_type=F32)


def _dot_nt(a, b):
    return lax.dot_general(a, b, (((1,), (1,)), ((), ())), preferred_element_type=F32)


Q_WIDTH = NSA_HEADS * HEAD_DIM
COL_Q = 0
COL_KV = COL_Q + Q_WIDTH
COL_U = COL_KV + 6 * KV_PAIR
COL_V = COL_U + SGU_WIDTH
COL_MERGE = COL_V + SGU_WIDTH
ROPED_KV = (0, 2, 4)
COMPRESSED_KV = (0, 1)
LOG2_E = 1.4426950408889634


def _inproj_kernel(x_ref, pos_ref, freq_ref, w_ref, lng_ref, lnb_ref, wsp_ref, bsp_ref,
                   q_ref, kc_ref, vc_ref, ks_ref, vs_ref, kw_ref, vw_ref,
                   gates_ref, sgu_ref, merge_ref, stage_ref, *, d_model):
    col_gates = COL_MERGE + 2 * d_model
    lane = lax.broadcasted_iota(jnp.int32, (1, LANES), 1)
    rest = HEAD_DIM - ROPE_DIM
    scale = HEAD_DIM ** -0.5 * LOG2_E
    kv_refs = (kc_ref, vc_ref, ks_ref, vs_ref, kw_ref, vw_ref)
    row = lax.broadcasted_iota(jnp.int32, (SGU_CHUNK, SGU_CHUNK), 0)
    col = lax.broadcasted_iota(jnp.int32, (SGU_CHUNK, SGU_CHUNK), 1)
    gdim = SGU_WIDTH // SGU_GROUPS
    w_sp = [jnp.where(col <= row, wsp_ref[g], 0.0).astype(BF16) for g in range(SGU_GROUPS)]

    sub = x_ref.shape[0] // INPROJ_SUB_BLOCKS
    for blk in range(INPROJ_SUB_BLOCKS):
        rows = slice(blk * sub, (blk + 1) * sub)
        xb = x_ref[rows, :].astype(BF16)

        u = jax.nn.gelu(_dot(xb, w_ref[:, COL_U:COL_U + SGU_WIDTH]))
        v = jax.nn.gelu(_dot(xb, w_ref[:, COL_V:COL_V + SGU_WIDTH]))
        v = _layer_norm(v, lng_ref[...], lnb_ref[...]).astype(BF16)

        rm = _dot(xb, w_ref[:, COL_MERGE:COL_MERGE + 2 * d_model])
        merge_ref[rows, :] = jax.nn.sigmoid(rm).astype(merge_ref.dtype)
        rg = _dot(xb, w_ref[:, col_gates:col_gates + LANES])
        gates_ref[0, rows, :] = jax.nn.sigmoid(rg)

        ang = freq_ref[...] * pos_ref[0, :, rows]
        cos_f = jnp.cos(ang)
        sin_f = jnp.sin(ang)
        per_head = lambda parts: jnp.concatenate(parts * (LANES // HEAD_DIM), axis=0).T
        cos_t = per_head([cos_f, cos_f, jnp.ones((rest, sub), F32)])
        sin_lo = per_head([-sin_f, jnp.zeros((ROPE_HALF + rest, sub), F32)])
        sin_hi = per_head([jnp.zeros((ROPE_HALF, sub), F32), sin_f, jnp.zeros((rest, sub), F32)])

        def rope(t):
            return (t * cos_t + pltpu.roll(t, LANES - ROPE_HALF, axis=1) * sin_lo
                    + pltpu.roll(t, ROPE_HALF, axis=1) * sin_hi)

        rq = _dot(xb, w_ref[:, COL_Q:COL_Q + Q_WIDTH])
        for pair in range(NSA_HEADS // 2):
            t = rope(rq[:, pair * LANES:(pair + 1) * LANES]) * scale
            t_sw = pltpu.roll(t, HEAD_DIM, axis=1)
            for half in range(2):
                h = 2 * pair + half
                grp = h // NSA_GROUP_SIZE
                src = t if half == grp else t_sw
                keep = (lane >= HEAD_DIM) if grp == 1 else (lane < HEAD_DIM)
                q_ref[0, h, rows, :] = jnp.where(keep, src, 0.0).astype(q_ref.dtype)

        rkv = _dot(xb, w_ref[:, COL_KV:COL_KV + 6 * KV_PAIR])
        for i, ref in enumerate(kv_refs):
            t = rkv[:, i * KV_PAIR:(i + 1) * KV_PAIR]
            if i in ROPED_KV:
                t = rope(t)
            if i in COMPRESSED_KV:
                stage_ref[i] = t
                n_rows = sub // CMP_STRIDE
                slabs = [stage_ref[i, pl.ds(tok, n_rows, stride=CMP_STRIDE), :] for tok in range(CMP_STRIDE)]
                ref[0, blk * n_rows:(blk + 1) * n_rows, :] = jnp.concatenate(slabs, axis=1).astype(ref.dtype)
            else:
                ref[0, rows, :] = t.astype(ref.dtype)

        for n in range(sub // SGU_CHUNK):
            chunk = slice(n * SGU_CHUNK, (n + 1) * SGU_CHUNK)
            mixed = jnp.concatenate(
                [_dot(w_sp[g], v[chunk, g * gdim:(g + 1) * gdim]) for g in range(SGU_GROUPS)], axis=1)
            out_rows = slice(blk * sub + n * SGU_CHUNK, blk * sub + (n + 1) * SGU_CHUNK)
            sgu_ref[out_rows, :] = (u[chunk, :] * (mixed + bsp_ref[...])).astype(sgu_ref.dtype)


def _inproj(x, pos_rows, freqs, w_all, ln_g, ln_b, w_sp, b_sp):
    B, S, D = x.shape
    tm = TOKEN_TILE
    nt = S // tm
    wcols = w_all.shape[1]
    x2 = x.reshape(B * S, D)
    kv_shape = jax.ShapeDtypeStruct((B, S, KV_PAIR), BF16)
    kv_spec = pl.BlockSpec((1, tm, KV_PAIR), lambda b, i: (b, i, 0))
    cmp_shape = jax.ShapeDtypeStruct((B, S // CMP_STRIDE, CMP_STRIDE * KV_PAIR), BF16)
    cmp_spec = pl.BlockSpec((1, tm // CMP_STRIDE, CMP_STRIDE * KV_PAIR), lambda b, i: (b, i, 0))
    const2 = lambda b, i: (0, 0)
    return pl.pallas_call(
        functools.partial(_inproj_kernel, d_model=D),
        grid=(B, nt),
        in_specs=[
            pl.BlockSpec((tm, D), lambda b, i: (b * nt + i, 0)),
            pl.BlockSpec((1, 1, tm), lambda b, i: (b * nt + i, 0, 0)),
            pl.BlockSpec((ROPE_HALF, 1), const2),
            pl.BlockSpec((D, wcols), const2, pipeline_mode=pl.Buffered(1)),
            pl.BlockSpec((1, SGU_WIDTH), const2),
            pl.BlockSpec((1, SGU_WIDTH), const2),
            pl.BlockSpec((SGU_GROUPS, SGU_CHUNK, SGU_CHUNK), lambda b, i: (0, 0, 0)),
            pl.BlockSpec((SGU_CHUNK, SGU_WIDTH), const2),
        ],
        out_specs=[
            pl.BlockSpec((1, NSA_HEADS, tm, LANES), lambda b, i: (b, 0, i, 0)),
            cmp_spec, cmp_spec, kv_spec, kv_spec, kv_spec, kv_spec,
            pl.BlockSpec((1, tm, LANES), lambda b, i: (b, i, 0)),
            pl.BlockSpec((tm, SGU_WIDTH), lambda b, i: (b * nt + i, 0)),
            pl.BlockSpec((tm, 2 * D), lambda b, i: (b * nt + i, 0)),
        ],
        out_shape=[
            jax.ShapeDtypeStruct((B, NSA_HEADS, S, LANES), BF16),
            cmp_shape, cmp_shape, kv_shape, kv_shape, kv_shape, kv_shape,
            jax.ShapeDtypeStruct((B, S, LANES), F32),
            jax.ShapeDtypeStruct((B * S, SGU_WIDTH), BF16),
            jax.ShapeDtypeStruct((B * S, 2 * D), BF16),
        ],
        scratch_shapes=[pltpu.VMEM((len(COMPRESSED_KV), tm // INPROJ_SUB_BLOCKS, KV_PAIR), F32)],
        compiler_params=pltpu.CompilerParams(
            dimension_semantics=("parallel", "parallel"), vmem_limit_bytes=VMEM_LIMIT),
        name="inproj",
    )(x2, pos_rows, freqs, w_all, ln_g, ln_b, w_sp, b_sp)


def _compress_kernel(k_ref, v_ref, kpe_ref, kw1_ref, kw2_ref, vpe_ref, vw1_ref, vw2_ref,
                     ko_ref, vo_ref):
    ncp = k_ref.shape[1]

    def one(tok_ref, pe_ref, w1_ref, w2_ref, out_ref):
        a = tok_ref[0]
        width = pe_ref.shape[1]
        pe_first = jnp.broadcast_to(pe_ref[0:1, :], (8, width)).astype(BF16)
        pe_second = jnp.broadcast_to(pe_ref[1:2, :], (8, width)).astype(BF16)
        outs = []
        for g in range(NSA_KV_GROUPS):
            first = _dot(a, w1_ref[0, g])
            second = _dot(a, w1_ref[1, g])
            second = pltpu.roll(second, ncp - 1, axis=0)
            bias = (_dot(pe_first, w1_ref[0, g]) + _dot(pe_second, w1_ref[1, g]))[0:1]
            hid = jax.nn.silu(first + second + bias).astype(BF16)
            outs.append(_dot(hid, w2_ref[...]))
        out_ref[0] = jnp.concatenate(outs, axis=1).astype(out_ref.dtype)

    one(k_ref, kpe_ref, kw1_ref, kw2_ref, ko_ref)
    one(v_ref, vpe_ref, vw1_ref, vw2_ref, vo_ref)


def _compress(k_tok, v_tok, kpe, kw1, kw2, vpe, vw1, vw2):
    B, ncp, width = k_tok.shape
    tok_spec = pl.BlockSpec((1, ncp, width), lambda b: (b, 0, 0))
    pe_spec = pl.BlockSpec(kpe.shape, lambda b: (0, 0))
    w1_spec = pl.BlockSpec(kw1.shape, lambda b: (0, 0, 0, 0))
    w2_spec = pl.BlockSpec(kw2.shape, lambda b: (0, 0))
    out_spec = pl.BlockSpec((1, ncp, KV_PAIR), lambda b: (b, 0, 0))
    out_shape = jax.ShapeDtypeStruct((B, ncp, KV_PAIR), BF16)
    return pl.pallas_call(
        _compress_kernel,
        grid=(B,),
        in_specs=[tok_spec, tok_spec, pe_spec, w1_spec, w2_spec, pe_spec, w1_spec, w2_spec],
        out_specs=[out_spec, out_spec],
        out_shape=[out_shape, out_shape],
        compiler_params=pltpu.CompilerParams(
            dimension_semantics=("parallel",), vmem_limit_bytes=VMEM_LIMIT),
        name="compress",
    )(k_tok, v_tok, kpe, kw1, kw2, vpe, vw1, vw2)


KNOCKED_OUT = -(2.0 ** 127)


def _top_rows(s, row_idx, rounds):
    n = float(s.shape[0])
    row_idx = row_idx.astype(F32)
    for _ in range(rounds):
        m = jnp.max(s, axis=0, keepdims=True)
        first = jnp.min(jnp.where(s == m, row_idx, n), axis=0, keepdims=True)
        s = jnp.where(row_idx == first, KNOCKED_OUT, s)
    return s <= KNOCKED_OUT


def _nsa_kernel(q_ref, kc_ref, vc_ref, ks_ref, vs_ref, kw_ref, vw_ref, gates_ref,
                ovl_ref, onehot_ref, gexp_ref, tok_ref, cbias_ref, wbias_ref, tbias_ref,
                o_ref, lhs_ref, *, n_sel):
    R = NSA_GROUP_SIZE
    QB = Q_BLOCK
    ncp = kc_ref.shape[1]
    ns = ovl_ref.shape[1]
    c = pl.program_id(1)
    start = c * QB
    t_col = start + lax.broadcasted_iota(jnp.int32, (QB, 1), 0)
    lane = lax.broadcasted_iota(jnp.int32, (1, LANES), 1)

    row_valid = jnp.concatenate([t_col >= CMP_BLOCK - 1] * NSA_HEADS, axis=0)

    cur_row = (start + lax.broadcasted_iota(jnp.int32, (1, QB), 1)) // SEL_BLOCK
    j_row = lax.broadcasted_iota(jnp.int32, (ns, QB), 0)
    valid_t = j_row <= cur_row
    forced_t = (j_row == 0) | (j_row == cur_row) | (j_row == cur_row - 1)
    free_t = valid_t & jnp.logical_not(forced_t)

    win_start = pl.multiple_of(jnp.maximum(start - WINDOW, 0), QB)
    first_own_block = start // SEL_BLOCK

    H = NSA_HEADS
    q_all = q_ref[0].reshape(H * QB, LANES)
    ones = jnp.ones((WIN_KEYS, LANES), BF16)
    low = lane < HEAD_DIM

    def with_ones(v):
        return jnp.concatenate([v, ones[:v.shape[0]]], axis=1)

    def group_values(v):
        return [jnp.where(low, v, 1.0), jnp.where(low, 1.0, v)]

    group_rows = [slice(g * R * QB, (g + 1) * R * QB) for g in range(NSA_KV_GROUPS)]

    def by_group(f):
        return jnp.concatenate([f(rows) for rows in group_rows], axis=0)

    q_tok = jnp.concatenate([q_all, tok_ref[...]], axis=1)

    def masked_scores(keys, key_bias):
        k_aug = jnp.concatenate([keys, key_bias], axis=1)
        return by_group(lambda rows: _dot_nt(q_tok[rows], k_aug))

    cmp_rows = pl.multiple_of(ncp - c * (QB // CMP_STRIDE), 8)
    s_c = masked_scores(kc_ref[0], cbias_ref[pl.ds(cmp_rows, ncp), :].astype(BF16))
    e_c = jnp.exp2(s_c - jnp.max(s_c, axis=-1, keepdims=True))
    inv_c = jnp.where(row_valid, 1.0 / jnp.sum(e_c, axis=-1, keepdims=True), 0.0)
    e_c = e_c.astype(BF16)
    v_ovl = jnp.concatenate([vc_ref[0], ovl_ref[...]], axis=1)
    pv_c = (by_group(lambda rows: _dot(e_c[rows], v_ovl)) * inv_c).reshape(H, QB, LANES + ns)
    o_cmp = pv_c[:, :, :LANES]
    both = lambda a: jnp.concatenate([a] * NSA_KV_GROUPS, axis=1)
    imp_t = jnp.concatenate([jnp.sum(pv_c[g * R:(g + 1) * R, :, LANES:], axis=0).T
                             for g in range(NSA_KV_GROUPS)], axis=1)
    picked_t = both(forced_t) | (_top_rows(jnp.where(both(free_t), imp_t, NEG_INF), both(j_row),
                                           n_sel - N_FORCED) & both(valid_t))
    bias_t = jnp.where(picked_t & (both(j_row) < first_own_block), 0.0, NEG_INF)
    bias = []
    for g in range(NSA_KV_GROUPS):
        bias += [bias_t[:, g * QB:(g + 1) * QB].T.astype(BF16)] * R

    s_wd = masked_scores(
        jnp.concatenate([kw_ref[0, pl.ds(win_start, WIN_KEYS), :], ks_ref[0, pl.ds(start, QB), :]], axis=0),
        jnp.concatenate([wbias_ref[jnp.minimum(c, WINDOW // QB)], tbias_ref[...]], axis=0))
    s_w, s_d = s_wd[:, :WIN_KEYS], s_wd[:, WIN_KEYS:]
    e_w = jnp.exp2(s_w - jnp.max(s_w, axis=-1, keepdims=True))
    e_w = e_w.astype(BF16)
    vw1 = with_ones(vw_ref[0, pl.ds(win_start, WIN_KEYS), :])
    acc_win = by_group(lambda rows: _dot(e_w[rows], vw1))

    lhs_ref[:, :LANES] = q_all
    lhs_ref[:, LANES:] = jnp.concatenate(bias, axis=0)
    m_run = jnp.max(s_d, axis=-1, keepdims=True)
    p_d = jnp.exp2(s_d - m_run).astype(BF16)

    v_d = group_values(vs_ref[0, pl.ds(start, QB), :])
    carry = []
    for g, rows in enumerate(group_rows):
        carry += [m_run[rows], _dot(p_d[rows], v_d[g])]

    gates = gates_ref[0].astype(BF16)
    g_cmp, g_sel, g_win = (_dot(gates, gexp_ref[k]) for k in range(3))

    def pair(x, r):
        x = x.reshape(H, QB, x.shape[-1])
        return jnp.where(low, x[r], x[R + r])

    tile_cols = [slice(r * LANES, (r + 1) * LANES) for r in range(R)]
    cmp_win = [g_cmp[:, tile_cols[r]] * pair(o_cmp, r)
               + g_win[:, tile_cols[r]] * (pair(acc_win[:, :LANES], r) * (1.0 / pair(acc_win[:, LANES:], r)))
               for r in range(R)]

    def sel_body(kb, carry):
        k0 = pl.multiple_of(kb * SEL_KEY_CHUNK, SEL_KEY_CHUNK)
        ke = jnp.concatenate([ks_ref[0, pl.ds(k0, SEL_KEY_CHUNK), :],
                              onehot_ref[pl.ds(k0, SEL_KEY_CHUNK), :]], axis=1)
        s = _dot_nt(lhs_ref[...], ke)
        v = group_values(vs_ref[0, pl.ds(k0, SEL_KEY_CHUNK), :])
        out = []
        for g, rows in enumerate(group_rows):
            m_run, acc = carry[2 * g:2 * g + 2]
            m_new = jnp.maximum(m_run, jnp.max(s[rows], axis=-1, keepdims=True))
            p = jnp.exp2(s[rows] - m_new).astype(BF16)
            out += [m_new, jnp.exp2(m_run - m_new) * acc + _dot(p, v[g])]
        return tuple(out)

    def sel_body_pair(kp, carry):
        return sel_body(2 * kp + 1, sel_body(2 * kp, carry))

    n_chunks = (start + SEL_KEY_CHUNK - 1) // SEL_KEY_CHUNK
    carry = lax.fori_loop(0, n_chunks // 2, sel_body_pair, tuple(carry))
    carry = lax.fori_loop(n_chunks - n_chunks % 2, n_chunks, sel_body, carry)

    sel0 = carry[1].reshape(R, QB, LANES)
    sel1 = carry[3].reshape(R, QB, LANES)
    tiles = []
    for r in range(R):
        sums = pltpu.roll(jnp.where(low, sel1[r], sel0[r]), HEAD_DIM, axis=1)
        o_sel = jnp.where(low, sel0[r], sel1[r]) * (1.0 / sums)
        tiles.append(cmp_win[r] + g_sel[:, tile_cols[r]] * o_sel)
    o_ref[0] = jnp.concatenate(tiles, axis=1).astype(o_ref.dtype)


def _nsa(q, kc, vc, ks, vs, kw, vw, gates, consts):
    B, H, S, _ = q.shape
    ncp = kc.shape[1]
    ns = consts[0].shape[1]
    nq = S // Q_BLOCK
    whole = lambda n: pl.BlockSpec((1, n, KV_PAIR), lambda b, c: (b, 0, 0))
    return pl.pallas_call(
        functools.partial(_nsa_kernel, n_sel=min(SEL_TOP_N, ns)),
        grid=(B, nq),
        in_specs=[
            pl.BlockSpec((1, H, Q_BLOCK, LANES), lambda b, c: (b, 0, c, 0)),
            whole(ncp), whole(ncp), whole(S), whole(S), whole(S), whole(S),
            pl.BlockSpec((1, Q_BLOCK, LANES), lambda b, c: (b, c, 0)),
        ] + [pl.BlockSpec(a.shape, lambda b, c, nd=a.ndim: (0,) * nd) for a in consts],
        out_specs=pl.BlockSpec((1, Q_BLOCK, H * HEAD_DIM), lambda b, c: (b, c, 0)),
        out_shape=jax.ShapeDtypeStruct((B, S, H * HEAD_DIM), BF16),
        scratch_shapes=[pltpu.VMEM((H * Q_BLOCK, LANES + ns), BF16)],
        compiler_params=pltpu.CompilerParams(
            dimension_semantics=("parallel", "arbitrary"), vmem_limit_bytes=VMEM_LIMIT),
        name="nsa",
    )(q, kc, vc, ks, vs, kw, vw, gates, *consts)


def _merge_kernel(x_ref, nsa_ref, sgu_ref, gate_ref, wbn_ref, wbs_ref, wout_ref, g_ref, b_ref,
                  o_ref, *, alpha):
    tm, d = x_ref.shape
    sub = tm // MERGE_SUB_BLOCKS
    for i in range(MERGE_SUB_BLOCKS):
        rows = slice(i * sub, (i + 1) * sub)
        a = _dot(nsa_ref[rows, :], wbn_ref[...])
        s = _dot(sgu_ref[rows, :], wbs_ref[...])
        merged = gate_ref[rows, :d].astype(F32) * a + gate_ref[rows, d:].astype(F32) * s
        mix = _dot(merged.astype(BF16), wout_ref[...])
        o_ref[rows, :] = _layer_norm(alpha * x_ref[rows, :] + mix, g_ref[...], b_ref[...])


def _merge(x2, o_nsa, o_sgu, merge_g, wbn, wbs, wout, ln_g, ln_b, alpha):
    T, D = x2.shape
    tm = MERGE_TOKEN_TILE
    row = lambda w: pl.BlockSpec((tm, w), lambda i: (i, 0))
    full = lambda a: pl.BlockSpec(a.shape, lambda i: (0, 0))
    return pl.pallas_call(
        functools.partial(_merge_kernel, alpha=alpha),
        grid=(T // tm,),
        in_specs=[row(D), row(o_nsa.shape[1]), row(o_sgu.shape[1]), row(2 * D),
                  full(wbn), full(wbs), full(wout), full(ln_g), full(ln_b)],
        out_specs=row(D),
        out_shape=jax.ShapeDtypeStruct((T, D), F32),
        compiler_params=pltpu.CompilerParams(
            dimension_semantics=("parallel",), vmem_limit_bytes=VMEM_LIMIT),
        name="merge",
    )(x2, o_nsa, o_sgu, merge_g, wbn, wbs, wout, ln_g, ln_b)


def _ffn_kernel(h_ref, wg_ref, wu_ref, wd_ref, g_ref, b_ref, o_ref, *, alpha):
    sub = h_ref.shape[0] // FFN_SUB_BLOCKS
    for i in range(FFN_SUB_BLOCKS):
        rows = slice(i * sub, (i + 1) * sub)
        h = h_ref[rows, :]
        hb = h.astype(BF16)
        act = jax.nn.silu(_dot(hb, wg_ref[...])) * _dot(hb, wu_ref[...])
        ffn = _dot(act.astype(BF16), wd_ref[...])
        o_ref[rows, :] = _layer_norm(alpha * h + ffn, g_ref[...], b_ref[...])


def _ffn(h, wg, wu, wd, ln_g, ln_b, alpha):
    T, D = h.shape
    tm = FFN_TOKEN_TILE
    resident = lambda a: pl.BlockSpec(a.shape, lambda i: (0, 0), pipeline_mode=pl.Buffered(1))
    return pl.pallas_call(
        functools.partial(_ffn_kernel, alpha=alpha),
        grid=(T // tm,),
        in_specs=[pl.BlockSpec((tm, D), lambda i: (i, 0)), resident(wg), resident(wu), resident(wd),
                  resident(ln_g), resident(ln_b)],
        out_specs=pl.BlockSpec((tm, D), lambda i: (i, 0)),
        out_shape=jax.ShapeDtypeStruct((T, D), F32),
        compiler_params=pltpu.CompilerParams(
            dimension_semantics=("parallel",), vmem_limit_bytes=VMEM_LIMIT),
        name="ffn",
    )(h, wg, wu, wd, ln_g, ln_b)


def _merge_ffn_kernel(x_ref, nsa_ref, sgu_ref, gate_ref, wbn_ref, wbs_ref, wout_ref, g1_ref, b1_ref,
                      wg_ref, wu_ref, wd_ref, g2_ref, b2_ref, o_ref, *, alpha):
    tm, d = x_ref.shape
    sub = tm // FUSED_SUB_BLOCKS
    for i in range(FUSED_SUB_BLOCKS):
        rows = slice(i * sub, (i + 1) * sub)
        a = _dot(nsa_ref[rows, :], wbn_ref[...])
        s = _dot(sgu_ref[rows, :], wbs_ref[...])
        merged = gate_ref[rows, :d].astype(F32) * a + gate_ref[rows, d:].astype(F32) * s
        mix = _dot(merged.astype(BF16), wout_ref[...])
        h = _layer_norm(alpha * x_ref[rows, :] + mix, g1_ref[...], b1_ref[...])
        hb = h.astype(BF16)
        act = jax.nn.silu(_dot(hb, wg_ref[...])) * _dot(hb, wu_ref[...])
        ffn = _dot(act.astype(BF16), wd_ref[...])
        o_ref[rows, :] = _layer_norm(alpha * h + ffn, g2_ref[...], b2_ref[...])


def _merge_ffn(x2, o_nsa, o_sgu, merge_g, wbn, wbs, wout, ln1_g, ln1_b, wg, wu, wd, ln2_g, ln2_b, alpha):
    T, D = x2.shape
    tm = FUSED_TOKEN_TILE
    row = lambda w: pl.BlockSpec((tm, w), lambda i: (i, 0))
    resident = lambda a: pl.BlockSpec(a.shape, lambda i: (0, 0), pipeline_mode=pl.Buffered(1))
    consts = (wbn, wbs, wout, ln1_g, ln1_b, wg, wu, wd, ln2_g, ln2_b)
    return pl.pallas_call(
        functools.partial(_merge_ffn_kernel, alpha=alpha),
        grid=(T // tm,),
        in_specs=[row(D), row(o_nsa.shape[1]), row(o_sgu.shape[1]), row(2 * D)] + [resident(a) for a in consts],
        out_specs=row(D),
        out_shape=jax.ShapeDtypeStruct((T, D), F32),
        compiler_params=pltpu.CompilerParams(
            dimension_semantics=("parallel",), vmem_limit_bytes=VMEM_LIMIT),
        name="merge_ffn",
    )(x2, o_nsa, o_sgu, merge_g, *consts)


def _split_sizes(d_model):
    return [Q_WIDTH] + [KV_PAIR] * 6 + [3 * NSA_HEADS, SGU_WIDTH, SGU_WIDTH, 2 * d_model]


def _nsa_constants(S):
    ncp = S // CMP_STRIDE
    ns = S // SEL_BLOCK
    QB = Q_BLOCK
    cs = np.arange(ncp)[:, None] * CMP_STRIDE
    ss = np.arange(ns)[None, :] * SEL_BLOCK
    overlap = np.clip(np.minimum(cs + CMP_BLOCK, ss + SEL_BLOCK) - np.maximum(cs, ss), 0, None) / CMP_BLOCK
    onehot = (np.arange(S)[:, None] // SEL_BLOCK) == np.arange(ns)[None, :]
    gate_expand = np.zeros((3, LANES, Q_WIDTH), np.float32)
    for head in range(NSA_HEADS):
        grp, r = divmod(head, NSA_GROUP_SIZE)
        lo = r * LANES + grp * HEAD_DIM
        for k in range(3):
            gate_expand[k, 3 * head + k, lo:lo + HEAD_DIM] = 1.0
    tok = np.tile(np.eye(QB, dtype=np.float32), (NSA_HEADS, 1))
    tl = np.arange(QB)[None, :]
    bias = lambda visible: np.where(visible, 0.0, NEG_INF).astype(np.float32)
    i_rel = np.arange(2 * ncp)[:, None] - ncp
    cmp_bias = bias(CMP_STRIDE * i_rel + CMP_BLOCK - 1 <= tl)
    kk = np.arange(WIN_KEYS)[:, None]
    n_clip = WINDOW // QB
    win_bias = np.stack([bias(kk <= c * QB + tl) for c in range(n_clip)]
                        + [bias((kk <= WINDOW + tl) & (kk > tl))])
    own_bias = bias(np.arange(QB)[:, None] <= tl)
    as_bf16 = lambda a: jnp.asarray(a, dtype=BF16)
    return (as_bf16(overlap), as_bf16(onehot), as_bf16(gate_expand), as_bf16(tok),
            jnp.asarray(cmp_bias), as_bf16(win_bias), as_bf16(own_bias))


def _compress_weights(w1):
    hid = w1.shape[1]
    w = w1.reshape(2, CMP_STRIDE, 1, HEAD_DIM, hid)
    eye = jnp.eye(NSA_KV_GROUPS, dtype=w1.dtype)
    wg = w[:, None] * eye[None, :, None, :, None, None]
    return wg.reshape(2, NSA_KV_GROUPS, CMP_STRIDE * KV_PAIR, hid).astype(BF16)


def _compress_pe(pe):
    p = jnp.broadcast_to(pe.reshape(2, CMP_STRIDE, 1, HEAD_DIM), (2, CMP_STRIDE, NSA_KV_GROUPS, HEAD_DIM))
    return p.reshape(2, CMP_STRIDE * KV_PAIR)


def kernel(x, positions, w_in, pe_ck, w_ck1, w_ck2, pe_cv, w_cv1, w_cv2, ln_sgu_g, ln_sgu_b,
           w_spatial, b_spatial, w_branch_nsa, w_branch_sgu, w_out, ln1_g, ln1_b,
           w_ffn_gate, w_ffn_up, w_ffn_down, ln2_g, ln2_b):
    B, S, D = x.shape
    depth = w_in.shape[0]
    alpha = (2.0 * depth) ** 0.25
    assert S % TOKEN_TILE == 0 and S % SEL_KEY_CHUNK == 0 and S >= WIN_KEYS
    ncp = S // CMP_STRIDE
    ns = S // SEL_BLOCK

    freqs = ROPE_THETA ** (-jnp.arange(ROPE_HALF, dtype=F32) / ROPE_HALF)
    freqs = freqs.reshape(ROPE_HALF, 1)
    pos_rows = positions.astype(F32).reshape(B * S // TOKEN_TILE, 1, TOKEN_TILE)

    nsa_consts = _nsa_constants(S)

    sizes = _split_sizes(D)
    offs = np.concatenate([[0], np.cumsum(sizes)])
    seg = lambda w, i: w[:, offs[i]:offs[i + 1]]

    h = x.reshape(B * S, D)
    for l in range(depth):
        w = w_in[l]
        gate_cols = jnp.pad(seg(w, 7), ((0, 0), (0, LANES - sizes[7])))
        w_all = jnp.concatenate([seg(w, i) for i in (0, 1, 2, 3, 4, 5, 6, 8, 9, 10)] + [gate_cols],
                                axis=1).astype(BF16)
        bsp = jnp.repeat(b_spatial[l].T, SGU_WIDTH // SGU_GROUPS, axis=1)
        q, k_c, v_c, k_s, v_s, k_w, v_w, gates, o_sgu, merge_g = _inproj(
            h.reshape(B, S, D), pos_rows, freqs, w_all,
            ln_sgu_g[l].reshape(1, -1), ln_sgu_b[l].reshape(1, -1), w_spatial[l], bsp)

        kc, vc = _compress(
            k_c, v_c,
            _compress_pe(pe_ck[l]), _compress_weights(w_ck1[l]), w_ck2[l].astype(BF16),
            _compress_pe(pe_cv[l]), _compress_weights(w_cv1[l]), w_cv2[l].astype(BF16))

        o_nsa = _nsa(q, kc, vc, k_s, v_s, k_w, v_w, gates, nsa_consts)

        wbn = w_branch_nsa[l].reshape(NSA_KV_GROUPS, NSA_GROUP_SIZE, HEAD_DIM, -1)
        wbn = wbn.transpose(1, 0, 2, 3).reshape(Q_WIDTH, -1)
        h = _merge_ffn(h, o_nsa.reshape(B * S, -1), o_sgu, merge_g,
                       wbn.astype(BF16), w_branch_sgu[l].astype(BF16), w_out[l].astype(BF16),
                       ln1_g[l].reshape(1, -1), ln1_b[l].reshape(1, -1),
                       w_ffn_gate[l].astype(BF16), w_ffn_up[l].astype(BF16), w_ffn_down[l].astype(BF16),
                       ln2_g[l].reshape(1, -1), ln2_b[l].reshape(1, -1), alpha)
    return h.reshape(B, S, D)
```
